```python
import jax, jax.numpy as jnp
from jax import lax
import numpy as np

D_MODEL = 2048
BATCH = 2
SEQ = 16384
DEPTH = 2

GRID_W = 64
CTX_LEN = 256
RET_HEADS = 8
RET_QK_DIM = D_MODEL // RET_HEADS
RET_V_DIM = D_MODEL // RET_HEADS
RET_WIDTH = RET_HEADS * RET_QK_DIM
RET_CHUNK = 128
ROPE_HALF = RET_QK_DIM // 2
ROPE_FREQS = ROPE_HALF // 2
ROPE_BASE = 10000.0
LRU_WIDTH = D_MODEL
LRU_BLOCKS = 8
LRU_BLOCK_DIM = LRU_WIDTH // LRU_BLOCKS
LRU_C = 8.0
CONV_W = 4
CONV_PAD = (2, 1)
FFN_HIDDEN = ((8 * D_MODEL // 3 + 255) // 256) * 256
NORM_EPS = 1e-6
IN_SIZES = (RET_WIDTH,) * 4 + (LRU_WIDTH,) * 2 + (D_MODEL,) * 2
IN_COLS = sum(IN_SIZES)

kernel_name = 'hybrid_retention_rglru_diffusion_block'


def _normal(key, shape, scale):
    return jax.random.normal(key, shape, jnp.float32) * scale


def rms_norm(x, g):
    xf = x.astype(jnp.float32)
    y = xf * lax.rsqrt(jnp.mean(xf * xf, axis=-1, keepdims=True) + NORM_EPS)
    return (y * g.astype(jnp.float32)).astype(x.dtype)


def head_norm(o, g):
    b, h, n, dv = o.shape
    of = o.astype(jnp.float32)
    mu = jnp.mean(of, axis=-1, keepdims=True)
    var = jnp.mean(jnp.square(of - mu), axis=-1, keepdims=True)
    y = (of - mu) * lax.rsqrt(var + NORM_EPS)
    y = y.transpose(0, 2, 1, 3).reshape(b, n, h * dv) * g.astype(jnp.float32)
    return y.astype(o.dtype)


def to_heads(t):
    b, n, _ = t.shape
    return t.reshape(b, n, RET_HEADS, -1).transpose(0, 2, 1, 3)


def rope_tables(row, col):
    inv = ROPE_BASE ** (-jnp.arange(ROPE_FREQS, dtype=jnp.float32) / ROPE_FREQS)
    ang_r = row[:, None] * inv[None, :]
    ang_c = col[:, None] * inv[None, :]
    return (jnp.cos(ang_r), jnp.sin(ang_r), jnp.cos(ang_c), jnp.sin(ang_c))


def apply_rope_2d(t, cos_r, sin_r, cos_c, sin_c):
    def rot(p, cs, sn):
        cs = cs.astype(p.dtype)
        sn = sn.astype(p.dtype)
        p1, p2 = p[..., :ROPE_FREQS], p[..., ROPE_FREQS:]
        return jnp.concatenate([p1 * cs - p2 * sn, p1 * sn + p2 * cs], axis=-1)
    return jnp.concatenate([rot(t[..., :ROPE_HALF], cos_r, sin_r),
                            rot(t[..., ROPE_HALF:], cos_c, sin_c)], axis=-1)


def retention_scan(q, k, v, log_g, s0):
    b, h, n, _ = q.shape
    dv = v.shape[-1]
    nc = n // RET_CHUNK

    def chunks(t):
        return t.reshape(b, h, nc, RET_CHUNK, t.shape[-1]).transpose(2, 0, 1, 3, 4)

    idx = jnp.arange(RET_CHUNK, dtype=jnp.float32)
    diff = idx[:, None] - idx[None, :]
    lg = log_g.astype(jnp.float32)
    causal = diff >= 0
    decay_in = jnp.where(causal[None], jnp.exp(jnp.where(causal, diff, 0.0)[None] * lg[:, None, None]), 0.0).astype(q.dtype)
    decay_q = jnp.exp((idx + 1.0)[None, :] * lg[:, None]).astype(q.dtype)
    decay_k = jnp.exp((RET_CHUNK - 1.0 - idx)[None, :] * lg[:, None]).astype(q.dtype)
    decay_s = jnp.exp(RET_CHUNK * lg).astype(q.dtype)

    def step(s, qkv):
        qc, kc, vc = qkv
        att = jnp.einsum('bhid,bhjd->bhij', qc, kc) * decay_in
        o = (jnp.einsum('bhij,bhje->bhie', att, vc)
             + jnp.einsum('bhid,bhde->bhie', qc, s) * decay_q[None, :, :, None])
        s = s * decay_s[None, :, None, None] + jnp.einsum('bhjd,bhje->bhde', kc * decay_k[None, :, :, None], vc)
        return s, o

    s_fin, o = lax.scan(step, s0, (chunks(q), chunks(k), chunks(v)))
    o = o.transpose(1, 2, 0, 3, 4).reshape(b, h, n, dv)
    return o, s_fin


def bidir_retention(q, k, v, log_g, s0_f, s0_b):
    o_f, s_f = retention_scan(q, k, v, log_g[0], s0_f)
    fl = lambda t: jnp.flip(t, axis=2)
    o_b, s_b = retention_scan(fl(q), fl(k), fl(v), log_g[1], s0_b)
    return o_f + fl(o_b), s_f, s_b


def dwconv(x, w, bias):
    y = lax.conv_general_dilated(x, w[:, None, :].astype(x.dtype), window_strides=(1,), padding=[CONV_PAD],
                                 dimension_numbers=('NWC', 'WIO', 'NWC'), feature_group_count=x.shape[-1])
    return y + bias.astype(x.dtype)


def rglru_scan(x, w_a, b_a, w_x, b_x, lam, h0):
    xf = x.astype(jnp.float32)
    b, n, w = xf.shape
    xb = xf.reshape(b, n, LRU_BLOCKS, LRU_BLOCK_DIM)
    r = jax.nn.sigmoid(jnp.einsum('bnki,kij->bnkj', xb, w_a.astype(jnp.float32)).reshape(b, n, w) + b_a.astype(jnp.float32))
    i = jax.nn.sigmoid(jnp.einsum('bnki,kij->bnkj', xb, w_x.astype(jnp.float32)).reshape(b, n, w) + b_x.astype(jnp.float32))
    log_a = -LRU_C * r * jax.nn.softplus(-lam.astype(jnp.float32))
    a = jnp.exp(log_a)
    u = jnp.sqrt(-jnp.expm1(2.0 * log_a)) * (i * xf)
    u = u.at[:, 0].add(a[:, 0] * h0)

    def comb(lhs, rhs):
        a1, b1 = lhs
        a2, b2 = rhs
        return a1 * a2, a2 * b1 + b2

    _, h = lax.associative_scan(comb, (a, u), axis=1)
    return h, h[:, -1]


def bidir_rglru(x, w_a, b_a, w_x, b_x, lam, h0_f, h0_b):
    h_f, hf_fin = rglru_scan(x, w_a[0], b_a[0], w_x[0], b_x[0], lam[0], h0_f)
    h_b, hb_fin = rglru_scan(jnp.flip(x, axis=1), w_a[1], b_a[1], w_x[1], b_x[1], lam[1], h0_b)
    return h_f + jnp.flip(h_b, axis=1), hf_fin, hb_fin


def token_mixer(h_lat, h_ctx, rope, w_in, b_in, ret_decay, ret_gn, conv_w, conv_b,
                lru_wa, lru_ba, lru_wx, lru_bx, lru_lambda, w_ret_o, w_lru_o, w_out, with_ctx):
    splits = [int(s) for s in np.cumsum(IN_SIZES)[:-1]]
    pl = jnp.split(h_lat @ w_in + b_in, splits, axis=-1)
    pc = jnp.split(h_ctx @ w_in + b_in, splits, axis=-1)
    b = h_lat.shape[0]

    log_g = jax.nn.log_sigmoid(ret_decay.astype(jnp.float32))
    k_scale = RET_QK_DIM ** -0.5
    qc, kc, vc = to_heads(pc[0]), to_heads(pc[1]) * k_scale, to_heads(pc[2])
    ql = apply_rope_2d(to_heads(pl[0]), *rope)
    kl = apply_rope_2d(to_heads(pl[1]), *rope) * k_scale
    vl = to_heads(pl[2])
    s0 = jnp.zeros((b, RET_HEADS, RET_QK_DIM, RET_V_DIM), qc.dtype)
    o_c, s_f, s_b = bidir_retention(qc, kc, vc, log_g, s0, s0)
    o_l, _, _ = bidir_retention(ql, kl, vl, log_g, s_f, s_b)

    xc = dwconv(pc[4], conv_w, conv_b)
    xl = dwconv(pl[4], conv_w, conv_b)
    h0 = jnp.zeros((b, LRU_WIDTH), jnp.float32)
    r_c, hf, hb = bidir_rglru(xc, lru_wa, lru_ba, lru_wx, lru_bx, lru_lambda, h0, h0)
    r_l, _, _ = bidir_rglru(xl, lru_wa, lru_ba, lru_wx, lru_bx, lru_lambda, hf, hb)

    def merge(o, p, r):
        y_r = jax.nn.silu(p[3]) * head_norm(o, ret_gn)
        y_l = r.astype(p[5].dtype) * jax.nn.gelu(p[5])
        y = jax.nn.sigmoid(p[6]) * (y_r @ w_ret_o) + jax.nn.sigmoid(p[7]) * (y_l @ w_lru_o)
        return y @ w_out

    y_lat = merge(o_l, pl, r_l)
    y_ctx = merge(o_c, pc, r_c) if with_ctx else None
    return y_lat, y_ctx


def swiglu(h, w_ffn_in, w_ffn_out):
    a, g = jnp.split(h @ w_ffn_in, 2, axis=-1)
    return (jax.nn.silu(a) * g) @ w_ffn_out


def setup_inputs(seed: int = 0) -> dict:
    key = jax.random.key(seed)
    ks = jax.random.split(key, 26)
    f32 = jnp.float32
    base = 1.0 - 2.0 ** (-5.0 - jnp.arange(RET_HEADS, dtype=f32))
    ret_decay = (jnp.broadcast_to(jnp.log(base) - jnp.log1p(-base), (DEPTH, 2, RET_HEADS))
                 + _normal(ks[12], (DEPTH, 2, RET_HEADS), 0.01))
    u = jax.random.uniform(ks[20], (DEPTH, 2, LRU_WIDTH), f32, 0.9, 0.999)
    s = u ** (1.0 / LRU_C)
    lru_lambda = jnp.log(s) - jnp.log1p(-s)
    return {
        'x': _normal(ks[0], (BATCH, SEQ, D_MODEL), 1.0),
        'c': _normal(ks[1], (BATCH, D_MODEL), 1.0),
        'ctx': _normal(ks[2], (BATCH, CTX_LEN, D_MODEL), 1.0),
        'c_ctx': _normal(ks[3], (D_MODEL,), 1.0),
        'w_mod': _normal(ks[4], (DEPTH, D_MODEL, 6 * D_MODEL), 0.5 * D_MODEL ** -0.5),
        'b_mod': _normal(ks[5], (DEPTH, 6 * D_MODEL), 0.02),
        'g_pre_mix': 1.0 + _normal(ks[6], (DEPTH, D_MODEL), 0.02),
        'g_post_mix': 1.0 + _normal(ks[7], (DEPTH, D_MODEL), 0.02),
        'g_pre_ffn': 1.0 + _normal(ks[8], (DEPTH, D_MODEL), 0.02),
        'g_post_ffn': 1.0 + _normal(ks[9], (DEPTH, D_MODEL), 0.02),
        'w_in': _normal(ks[10], (DEPTH, D_MODEL, IN_COLS), D_MODEL ** -0.5),
        'b_in': _normal(ks[11], (DEPTH, IN_COLS), 0.02),
        'ret_decay': ret_decay,
        'ret_gn': 1.0 + _normal(ks[13], (DEPTH, RET_WIDTH), 0.02),
        'conv_w': _normal(ks[14], (DEPTH, CONV_W, LRU_WIDTH), CONV_W ** -0.5),
        'conv_b': _normal(ks[15], (DEPTH, LRU_WIDTH), 0.02),
        'lru_wa': _normal(ks[16], (DEPTH, 2, LRU_BLOCKS, LRU_BLOCK_DIM, LRU_BLOCK_DIM), LRU_BLOCK_DIM ** -0.5),
        'lru_ba': _normal(ks[17], (DEPTH, 2, LRU_WIDTH), 0.02),
        'lru_wx': _normal(ks[18], (DEPTH, 2, LRU_BLOCKS, LRU_BLOCK_DIM, LRU_BLOCK_DIM), LRU_BLOCK_DIM ** -0.5),
        'lru_bx': _normal(ks[19], (DEPTH, 2, LRU_WIDTH), 0.02),
        'lru_lambda': lru_lambda,
        'w_ret_o': _normal(ks[21], (DEPTH, RET_WIDTH, D_MODEL), RET_WIDTH ** -0.5),
        'w_lru_o': _normal(ks[22], (DEPTH, LRU_WIDTH, D_MODEL), LRU_WIDTH ** -0.5),
        'w_out': _normal(ks[23], (DEPTH, D_MODEL, D_MODEL), D_MODEL ** -0.5),
        'w_ffn_in': _normal(ks[24], (DEPTH, D_MODEL, 2 * FFN_HIDDEN), D_MODEL ** -0.5),
        'w_ffn_out': _normal(ks[25], (DEPTH, FFN_HIDDEN, D_MODEL), FFN_HIDDEN ** -0.5),
    }


def reference(x, c, ctx, c_ctx, w_mod, b_mod, g_pre_mix, g_post_mix, g_pre_ffn, g_post_ffn,
              w_in, b_in, ret_decay, ret_gn, conv_w, conv_b, lru_wa, lru_ba, lru_wx, lru_bx,
              lru_lambda, w_ret_o, w_lru_o, w_out, w_ffn_in, w_ffn_out):
    n = x.shape[1]
    rows = n // GRID_W
    row = jnp.repeat(jnp.arange(rows, dtype=jnp.float32), GRID_W)
    col = jnp.tile(jnp.arange(GRID_W, dtype=jnp.float32), rows)
    rope = rope_tables(row, col)
    for l in range(DEPTH):
        with_ctx = l < DEPTH - 1
        m_lat = (jax.nn.silu(c) @ w_mod[l] + b_mod[l])[:, None, :]
        m_ctx = jax.nn.silu(c_ctx) @ w_mod[l] + b_mod[l]
        sh1, sc1, ga1, sh2, sc2, ga2 = jnp.split(m_lat, 6, axis=-1)
        csh1, csc1, cga1, csh2, csc2, cga2 = jnp.split(m_ctx, 6, axis=-1)

        h_lat = rms_norm(x, g_pre_mix[l]) * (1.0 + sc1) + sh1
        h_ctx = rms_norm(ctx, g_pre_mix[l]) * (1.0 + csc1) + csh1
        y_lat, y_ctx = token_mixer(h_lat, h_ctx, rope, w_in[l], b_in[l], ret_decay[l], ret_gn[l],
                                   conv_w[l], conv_b[l], lru_wa[l], lru_ba[l], lru_wx[l], lru_bx[l],
                                   lru_lambda[l], w_ret_o[l], w_lru_o[l], w_out[l], with_ctx)
        x = x + ga1 * rms_norm(y_lat, g_post_mix[l])
        f_lat = swiglu(rms_norm(x, g_pre_ffn[l]) * (1.0 + sc2) + sh2, w_ffn_in[l], w_ffn_out[l])
        x = x + ga2 * rms_norm(f_lat, g_post_ffn[l])

        if with_ctx:
            ctx = ctx + cga1 * rms_norm(y_ctx, g_post_mix[l])
            f_ctx = swiglu(rms_norm(ctx, g_pre_ffn[l]) * (1.0 + csc2) + csh2, w_ffn_in[l], w_ffn_out[l])
            ctx = ctx + cga2 * rms_norm(f_ctx, g_post_ffn[l])
    return x
```

```python
import functools

import jax
import jax.numpy as jnp
from jax import lax
from jax.experimental import pallas as pl
from jax.experimental.pallas import tpu as pltpu

F32 = jnp.float32
BF16 = jnp.bfloat16

D_MODEL = 2048
GRID_W = 64
RET_HEADS = 8
HEAD_DIM = D_MODEL // RET_HEADS
ROPE_HALF = HEAD_DIM // 2
ROPE_FREQS = ROPE_HALF // 2
ROPE_BASE = 10000.0
LRU_BLOCKS = 8
LRU_BLOCK_DIM = D_MODEL // LRU_BLOCKS
LRU_C = 8.0
FFN_HIDDEN = ((8 * D_MODEL // 3 + 255) // 256) * 256
NORM_EPS = 1e-6
N_SEG = 8
IN_COLS = N_SEG * D_MODEL
K_SCALE = HEAD_DIM ** -0.5

RET_CHUNK = 256
VMEM_LIMIT = 56 * 1024 * 1024


def _cparams(sem):
    return pltpu.CompilerParams(dimension_semantics=sem, vmem_limit_bytes=VMEM_LIMIT)


def _rms(x, g):
    ms = jnp.mean(x * x, axis=-1, keepdims=True)
    return x * lax.rsqrt(ms + NORM_EPS) * g


def _gelu_tanh(x):
    c = 0.7978845608028654
    return 0.5 * x * (1.0 + jnp.tanh(c * (x + 0.044715 * (x * x * x))))


def _softplus(x):
    return jnp.maximum(x, 0.0) + jnp.log1p(jnp.exp(-jnp.abs(x)))


MOD_TN = 1024


def _mod_kernel(c_ref, w_ref, b_ref, o_ref):
    c = c_ref[...]
    s = (c * jax.nn.sigmoid(c)).astype(BF16)
    o_ref[...] = jnp.dot(s, w_ref[...].astype(BF16), preferred_element_type=F32) + b_ref[...]


def _modulation(c_rows, w_mod, b_mod):
    depth = w_mod.shape[0]
    rows = c_rows.shape[0]
    ncol = w_mod.shape[2]
    return pl.pallas_call(
        _mod_kernel,
        grid=(depth, ncol // MOD_TN),
        in_specs=[
            pl.BlockSpec((rows, D_MODEL), lambda l, j: (0, 0)),
            pl.BlockSpec((None, D_MODEL, MOD_TN), lambda l, j: (l, 0, j)),
            pl.BlockSpec((None, 1, MOD_TN), lambda l, j: (l, 0, j)),
        ],
        out_specs=pl.BlockSpec((None, rows, MOD_TN), lambda l, j: (l, 0, j)),
        out_shape=jax.ShapeDtypeStruct((depth, rows, ncol), F32),
        compiler_params=_cparams(("arbitrary", "arbitrary")),
        name="modulation",
    )(c_rows, w_mod, b_mod.reshape(depth, 1, ncol))


INPROJ_TN = 512
STEPS_PER_SEG = D_MODEL // INPROJ_TN


def _inproj_kernel(x_ref, sh_ref, sc_ref, g_ref, w_ref, b_ref, cos_ref, sin_ref, o_ref, h_scr,
                   *, use_rope):
    j = pl.program_id(1)

    @pl.when(j == 0)
    def _():
        y = _rms(x_ref[...], g_ref[...])
        h_scr[...] = (y * (1.0 + sc_ref[...]) + sh_ref[...]).astype(BF16)

    seg = j // STEPS_PER_SEG

    def proj():
        return jnp.dot(h_scr[...], w_ref[...], preferred_element_type=F32) + b_ref[...]

    def rope(t):
        cos = cos_ref[...]
        sin = sin_ref[...]
        outs = []
        for hh in range(INPROJ_TN // HEAD_DIM):
            for part in range(HEAD_DIM // ROPE_HALF):
                lo = hh * HEAD_DIM + part * ROPE_HALF
                tt = t[:, lo:lo + ROPE_HALF]
                cs = cos[:, part * ROPE_HALF:(part + 1) * ROPE_HALF]
                sn = sin[:, part * ROPE_HALF:(part + 1) * ROPE_HALF]
                outs.append(tt * cs + pltpu.roll(tt, ROPE_FREQS, 1) * sn)
        return jnp.concatenate(outs, axis=1)

    @pl.when(seg == 0)
    def _():
        t = proj()
        o_ref[...] = (rope(t) if use_rope else t).astype(BF16)

    @pl.when(seg == 1)
    def _():
        t = proj()
        o_ref[...] = ((rope(t) if use_rope else t) * K_SCALE).astype(BF16)

    @pl.when((seg == 2) | (seg == 4))
    def _():
        o_ref[...] = proj().astype(BF16)

    @pl.when(seg == 3)
    def _():
        t = proj()
        o_ref[...] = (t * jax.nn.sigmoid(t)).astype(BF16)

    @pl.when(seg == 5)
    def _():
        o_ref[...] = _gelu_tanh(proj()).astype(BF16)

    @pl.when(seg >= 6)
    def _():
        o_ref[...] = jax.nn.sigmoid(proj()).astype(BF16)


def _in_proj(x2, mods, l, g_pre, w_in, b_in, cos_tab, sin_tab, *, tm, tiles_per_group, group0,
             use_rope):
    m = x2.shape[0]
    nt = m // tm
    tab_tiles = cos_tab.shape[0] // tm

    def mod_map(chunk):
        return lambda i, j: (l, group0 + i // tiles_per_group, 0, chunk)

    return pl.pallas_call(
        functools.partial(_inproj_kernel, use_rope=use_rope),
        grid=(nt, IN_COLS // INPROJ_TN),
        in_specs=[
            pl.BlockSpec((tm, D_MODEL), lambda i, j: (i, 0)),
            pl.BlockSpec((None, None, 1, D_MODEL), mod_map(0)),
            pl.BlockSpec((None, None, 1, D_MODEL), mod_map(1)),
            pl.BlockSpec((1, D_MODEL), lambda i, j: (0, 0)),
            pl.BlockSpec((D_MODEL, INPROJ_TN), lambda i, j: (0, j)),
            pl.BlockSpec((1, INPROJ_TN), lambda i, j: (0, j)),
            pl.BlockSpec((tm, HEAD_DIM), lambda i, j: (i % tab_tiles, 0)),
            pl.BlockSpec((tm, HEAD_DIM), lambda i, j: (i % tab_tiles, 0)),
        ],
        out_specs=pl.BlockSpec((tm, INPROJ_TN), lambda i, j: (i, j)),
        out_shape=jax.ShapeDtypeStruct((m, IN_COLS), BF16),
        scratch_shapes=[pltpu.VMEM((tm, D_MODEL), BF16)],
        compiler_params=_cparams(("arbitrary", "arbitrary")),
        name="in_proj",
    )(x2, mods, mods, g_pre.reshape(1, D_MODEL), w_in, b_in.reshape(1, IN_COLS), cos_tab, sin_tab)


def _log_sigmoid(x):
    return jnp.minimum(x, 0.0) - jnp.log1p(jnp.exp(-jnp.abs(x)))


_TN_DIMS = (((0,), (0,)), ((), ()))
_NT_DIMS = (((1,), (1,)), ((), ()))


def _ret_bwd_kernel(dec_ref, k_ref, v_ref, s0_ref, sb_ref, sfin_ref, s_scr, dk_scr, ds_scr,
                    *, cpb):
    h = pl.program_id(1)
    j = pl.program_id(2)
    c = RET_CHUNK

    @pl.when(j == 0)
    def _():
        s_scr[...] = s0_ref[...]
        lg = _log_sigmoid(jnp.full((c, HEAD_DIM), dec_ref[1, h], F32))
        row = lax.broadcasted_iota(jnp.int32, (c, HEAD_DIM), 0).astype(F32)
        dk_scr[...] = jnp.exp(row * lg)
        ds_scr[...] = jnp.exp(float(c) * _log_sigmoid(jnp.full((8, HEAD_DIM), dec_ref[1, h], F32)))

    for cc in range(cpb - 1, -1, -1):
        sb_ref[cc] = s_scr[...].astype(BF16)
        k = k_ref[cc * c:(cc + 1) * c, :]
        v = v_ref[cc * c:(cc + 1) * c, :]
        vb = (v.astype(F32) * dk_scr[...]).astype(BF16)
        kv = lax.dot_general(k, vb, _TN_DIMS, preferred_element_type=F32)
        s_scr[...] = s_scr[...] * ds_scr[0:1, :] + kv

    @pl.when(j == pl.num_programs(2) - 1)
    def _():
        sfin_ref[...] = s_scr[...]


def _ret_fwd_kernel(dec_ref, q_ref, k_ref, v_ref, og_ref, sb_ref, s0_ref, gn_ref, y_ref, sfin_ref,
                    s_scr, mask_scr, dqf_scr, dqb_scr, dk_scr, ds_scr, *, cpb):
    h = pl.program_id(1)
    j = pl.program_id(2)
    c = RET_CHUNK

    @pl.when(j == 0)
    def _():
        s_scr[...] = s0_ref[...]
        lgf = _log_sigmoid(jnp.full((c, c), dec_ref[0, h], F32))
        lgb = _log_sigmoid(jnp.full((c, c), dec_ref[1, h], F32))
        ii = lax.broadcasted_iota(jnp.int32, (c, c), 0)
        jj = lax.broadcasted_iota(jnp.int32, (c, c), 1)
        d = (ii - jj).astype(F32)
        fwd = jnp.where(d >= 0.0, jnp.exp(jnp.maximum(d, 0.0) * lgf), 0.0)
        bwd = jnp.where(d <= 0.0, jnp.exp(jnp.maximum(-d, 0.0) * lgb), 0.0)
        mask_scr[...] = fwd + bwd
        row = ii.astype(F32)
        dqf_scr[...] = jnp.exp((row + 1.0) * lgf)
        dqb_scr[...] = jnp.exp((float(c) - row) * lgb)
        dk_scr[...] = jnp.exp((float(c) - 1.0 - row) * lgf)
        ds_scr[...] = jnp.exp(float(c) * _log_sigmoid(jnp.full((8, HEAD_DIM), dec_ref[0, h], F32)))

    for cc in range(cpb):
        sl = slice(cc * c, (cc + 1) * c)
        q = q_ref[sl, :]
        k = k_ref[sl, :]
        v = v_ref[sl, :]
        att = lax.dot_general(q, k, _NT_DIMS, preferred_element_type=F32)
        p = (att * mask_scr[...]).astype(BF16)
        o = jnp.dot(p, v, preferred_element_type=F32)
        o = o + dqf_scr[...] * jnp.dot(q, s_scr[...].astype(BF16), preferred_element_type=F32)
        o = o + dqb_scr[...] * jnp.dot(q, sb_ref[cc], preferred_element_type=F32)
        vf = (v.astype(F32) * dk_scr[...]).astype(BF16)
        kv = lax.dot_general(k, vf, _TN_DIMS, preferred_element_type=F32)
        s_scr[...] = s_scr[...] * ds_scr[0:1, :] + kv
        mu = jnp.mean(o, axis=-1, keepdims=True)
        oc = o - mu
        var = jnp.mean(oc * oc, axis=-1, keepdims=True)
        y = oc * lax.rsqrt(var + NORM_EPS) * gn_ref[...]
        y_ref[sl, :] = (y * og_ref[sl, :].astype(F32)).astype(BF16)

    @pl.when(j == pl.num_programs(2) - 1)
    def _():
        sfin_ref[...] = s_scr[...]


def _retention(proj, ret_decay, ret_gn, s0_f, s0_b, *, batch, n, cpb):
    c = RET_CHUNK
    rows = cpb * c
    nb = n // rows
    hd = HEAD_DIM
    state_spec = pl.BlockSpec((None, None, hd, hd), lambda b, h, j: (b, h, 0, 0))
    smem = pl.BlockSpec(memory_space=pltpu.SMEM)

    def col(seg, rev):
        if rev:
            return pl.BlockSpec((rows, hd), lambda b, h, j: (b * nb + nb - 1 - j, seg * RET_HEADS + h))
        return pl.BlockSpec((rows, hd), lambda b, h, j: (b * nb + j, seg * RET_HEADS + h))

    sb, sfin_b = pl.pallas_call(
        functools.partial(_ret_bwd_kernel, cpb=cpb),
        grid=(batch, RET_HEADS, nb),
        in_specs=[smem, col(1, True), col(2, True), state_spec],
        out_specs=[
            pl.BlockSpec((None, None, cpb, hd, hd), lambda b, h, j: (b, h, nb - 1 - j, 0, 0)),
            state_spec,
        ],
        out_shape=[
            jax.ShapeDtypeStruct((batch, RET_HEADS, nb * cpb, hd, hd), BF16),
            jax.ShapeDtypeStruct((batch, RET_HEADS, hd, hd), F32),
        ],
        scratch_shapes=[pltpu.VMEM((hd, hd), F32), pltpu.VMEM((c, hd), F32), pltpu.VMEM((8, hd), F32)],
        compiler_params=_cparams(("arbitrary", "arbitrary", "arbitrary")),
        name="ret_bwd",
    )(ret_decay, proj, proj, s0_b)

    y, sfin_f = pl.pallas_call(
        functools.partial(_ret_fwd_kernel, cpb=cpb),
        grid=(batch, RET_HEADS, nb),
        in_specs=[
            smem, col(0, False), col(1, False), col(2, False), col(3, False),
            pl.BlockSpec((None, None, cpb, hd, hd), lambda b, h, j: (b, h, j, 0, 0)),
            state_spec,
            pl.BlockSpec((1, hd), lambda b, h, j: (0, h)),
        ],
        out_specs=[
            pl.BlockSpec((rows, hd), lambda b, h, j: (b * nb + j, h)),
            state_spec,
        ],
        out_shape=[
            jax.ShapeDtypeStruct((batch * n, D_MODEL), BF16),
            jax.ShapeDtypeStruct((batch, RET_HEADS, hd, hd), F32),
        ],
        scratch_shapes=[
            pltpu.VMEM((hd, hd), F32), pltpu.VMEM((c, c), F32), pltpu.VMEM((c, hd), F32),
            pltpu.VMEM((c, hd), F32), pltpu.VMEM((c, hd), F32), pltpu.VMEM((8, hd), F32),
        ],
        compiler_params=_cparams(("arbitrary", "arbitrary", "arbitrary")),
        name="ret_fwd",
    )(ret_decay, proj, proj, proj, proj, sb, s0_f, ret_gn.reshape(1, D_MODEL))
    return y, sfin_f, sfin_b


HALO = 16
SUB = 8


def _lru_kernel(*refs, reverse, tt):
    if reverse:
        (xp_ref, x_ref, xn_ref, cw_ref, cb_ref, wa_ref, ba_ref, wx_ref, bx_ref, lam_ref, h0_ref,
         out_ref, hfin_ref, carry_scr, a_scr, u_scr, h_scr) = refs
    else:
        (xp_ref, x_ref, xn_ref, cw_ref, cb_ref, wa_ref, ba_ref, wx_ref, bx_ref, lam_ref, h0_ref,
         hb_ref, gate_ref, out_ref, hfin_ref, carry_scr, a_scr, u_scr, h_scr) = refs
    i = pl.program_id(2)
    nt = pl.num_programs(2)
    ti = nt - 1 - i if reverse else i
    w = LRU_BLOCK_DIM
    groups = tt // SUB

    @pl.when(i == 0)
    def _():
        carry_scr[...] = h0_ref[...]

    x = x_ref[...].astype(F32)
    prev = jnp.where(ti == 0, 0.0, xp_ref[HALO - 2:HALO, :].astype(F32))
    nxt = jnp.where(ti == nt - 1, 0.0, xn_ref[0:1, :].astype(F32))
    row = lax.broadcasted_iota(jnp.int32, (tt, w), 0)
    xm1 = jnp.where(row == 0, prev[1:2, :], pltpu.roll(x, 1, 0))
    xm2 = jnp.where(row == 0, prev[0:1, :], jnp.where(row == 1, prev[1:2, :], pltpu.roll(x, 2, 0)))
    xp1 = jnp.where(row == tt - 1, nxt, pltpu.roll(x, tt - 1, 0))
    cw = cw_ref[...]
    xc = cw[0:1, :] * xm2 + cw[1:2, :] * xm1 + cw[2:3, :] * x + cw[3:4, :] * xp1 + cb_ref[...]

    xb = xc.astype(BF16)
    r = jax.nn.sigmoid(jnp.dot(xb, wa_ref[...], preferred_element_type=F32) + ba_ref[...])
    ig = jax.nn.sigmoid(jnp.dot(xb, wx_ref[...], preferred_element_type=F32) + bx_ref[...])
    log_a = (-LRU_C) * r * _softplus(-lam_ref[...])
    a = jnp.exp(log_a)
    u = jnp.sqrt(1.0 - a * a) * (ig * xc)

    a3 = a.reshape(groups, SUB, w)
    u3 = u.reshape(groups, SUB, w)
    sub = lax.broadcasted_iota(jnp.int32, (groups, SUB, w), 1)
    for d in (1, 2, 4):
        if reverse:
            valid = sub < SUB - d
            shift = SUB - d
        else:
            valid = sub >= d
            shift = d
        a_sh = jnp.where(valid, pltpu.roll(a3, shift, 1), 1.0)
        u_sh = jnp.where(valid, pltpu.roll(u3, shift, 1), 0.0)
        u3 = a3 * u_sh + u3
        a3 = a3 * a_sh
    a_scr[...] = a3
    u_scr[...] = u3

    edge = 0 if reverse else SUB - 1

    def body(g, carry):
        gi = groups - 1 - g if reverse else g
        hg = u_scr[gi] + a_scr[gi] * carry
        h_scr[gi] = hg
        return hg[edge:edge + 1, :]

    carry = lax.fori_loop(0, groups, body, carry_scr[...], unroll=8)
    carry_scr[...] = carry
    hs = h_scr[...].reshape(tt, w)
    if reverse:
        out_ref[...] = hs.astype(BF16)
    else:
        out_ref[...] = ((hs + hb_ref[...].astype(F32)) * gate_ref[...].astype(F32)).astype(BF16)

    @pl.when(i == nt - 1)
    def _():
        hfin_ref[...] = carry


def _rglru(proj, conv_w, conv_b, lru_wa, lru_ba, lru_wx, lru_bx, lru_lambda, h0_f, h0_b,
           *, batch, n, tt):
    w = LRU_BLOCK_DIM
    nt = n // tt
    hpt = tt // HALO
    nh = n // HALO
    xseg = 4 * LRU_BLOCKS
    gseg = 5 * LRU_BLOCKS

    def specs(rev):
        def t_of(i):
            return nt - 1 - i if rev else i

        def vec(arr_rows):
            return pl.BlockSpec((arr_rows, w), lambda b, cb, i: (0, cb))

        return [
            pl.BlockSpec((HALO, w), lambda b, cb, i: (b * nh + jnp.maximum(t_of(i) * hpt - 1, 0), xseg + cb)),
            pl.BlockSpec((tt, w), lambda b, cb, i: (b * nt + t_of(i), xseg + cb)),
            pl.BlockSpec((HALO, w), lambda b, cb, i: (b * nh + jnp.minimum((t_of(i) + 1) * hpt, nh - 1), xseg + cb)),
            vec(4), vec(1),
            pl.BlockSpec((None, w, w), lambda b, cb, i: (cb, 0, 0)), vec(1),
            pl.BlockSpec((None, w, w), lambda b, cb, i: (cb, 0, 0)), vec(1),
            vec(1),
            pl.BlockSpec((None, 1, w), lambda b, cb, i: (b, 0, cb)),
        ]

    scratch = [pltpu.VMEM((1, w), F32)] + [pltpu.VMEM((tt // SUB, SUB, w), F32)] * 3
    state_spec = pl.BlockSpec((None, 1, w), lambda b, cb, i: (b, 0, cb))
    state_shape = jax.ShapeDtypeStruct((batch, 1, D_MODEL), F32)
    grid = (batch, LRU_BLOCKS, nt)
    params = _cparams(("arbitrary", "arbitrary", "arbitrary"))

    def small(d):
        return (conv_w, conv_b.reshape(1, -1), lru_wa[d], lru_ba[d].reshape(1, -1), lru_wx[d],
                lru_bx[d].reshape(1, -1), lru_lambda[d].reshape(1, -1))

    hb, hfin_b = pl.pallas_call(
        functools.partial(_lru_kernel, reverse=True, tt=tt),
        grid=grid,
        in_specs=specs(True),
        out_specs=[pl.BlockSpec((tt, w), lambda b, cb, i: (b * nt + nt - 1 - i, cb)), state_spec],
        out_shape=[jax.ShapeDtypeStruct((batch * n, D_MODEL), BF16), state_shape],
        scratch_shapes=scratch,
        compiler_params=params,
        name="lru_bwd",
    )(proj, proj, proj, *small(1), h0_b)

    y, hfin_f = pl.pallas_call(
        functools.partial(_lru_kernel, reverse=False, tt=tt),
        grid=grid,
        in_specs=specs(False) + [
            pl.BlockSpec((tt, w), lambda b, cb, i: (b * nt + i, cb)),
            pl.BlockSpec((tt, w), lambda b, cb, i: (b * nt + i, gseg + cb)),
        ],
        out_specs=[pl.BlockSpec((tt, w), lambda b, cb, i: (b * nt + i, cb)), state_spec],
        out_shape=[jax.ShapeDtypeStruct((batch * n, D_MODEL), BF16), state_shape],
        scratch_shapes=scratch,
        compiler_params=params,
        name="lru_fwd",
    )(proj, proj, proj, *small(0), h0_f, hb, proj)
    return y, hfin_f, hfin_b


MERGE_TM = 256


def _merge_kernel(yr_ref, yl_ref, sa_ref, sb_ref, x_ref, ga_ref, g_ref, wr_ref, wl_ref, wo_ref,
                  o_ref):
    pr = jnp.dot(yr_ref[...], wr_ref[...], preferred_element_type=F32)
    plru = jnp.dot(yl_ref[...], wl_ref[...], preferred_element_type=F32)
    y = sa_ref[...].astype(F32) * pr + sb_ref[...].astype(F32) * plru
    z = jnp.dot(y.astype(BF16), wo_ref[...], preferred_element_type=F32)
    o_ref[...] = x_ref[...] + ga_ref[...] * _rms(z, g_ref[...])


def _merge(y_r, y_l, proj, x2, mods, l, g_post, w_ret_o, w_lru_o, w_out, *, tiles_per_group, group0):
    m = x2.shape[0]
    tm = MERGE_TM
    row = lambda i: (i, 0)
    wspec = pl.BlockSpec((D_MODEL, D_MODEL), lambda i: (0, 0), pipeline_mode=pl.Buffered(1))
    return pl.pallas_call(
        _merge_kernel,
        grid=(m // tm,),
        in_specs=[
            pl.BlockSpec((tm, D_MODEL), row),
            pl.BlockSpec((tm, D_MODEL), row),
            pl.BlockSpec((tm, D_MODEL), lambda i: (i, 6)),
            pl.BlockSpec((tm, D_MODEL), lambda i: (i, 7)),
            pl.BlockSpec((tm, D_MODEL), row),
            pl.BlockSpec((None, None, 1, D_MODEL), lambda i: (l, group0 + i // tiles_per_group, 0, 2)),
            pl.BlockSpec((1, D_MODEL), lambda i: (0, 0)),
            wspec, wspec, wspec,
        ],
        out_specs=pl.BlockSpec((tm, D_MODEL), row),
        out_shape=jax.ShapeDtypeStruct((m, D_MODEL), F32),
        compiler_params=_cparams(("arbitrary",)),
        name="merge",
    )(y_r, y_l, proj, proj, x2, mods, g_post.reshape(1, D_MODEL), w_ret_o, w_lru_o, w_out)


FFN_TH = 512
FFN_STEPS = FFN_HIDDEN // FFN_TH


def _ffn_kernel(x_ref, sh_ref, sc_ref, ga_ref, gpre_ref, gpost_ref, wa_ref, wg_ref, wo_ref, o_ref,
                h_scr, acc_scr):
    j = pl.program_id(1)

    @pl.when(j == 0)
    def _():
        y = _rms(x_ref[...], gpre_ref[...])
        h_scr[...] = (y * (1.0 + sc_ref[...]) + sh_ref[...]).astype(BF16)
        acc_scr[...] = jnp.zeros_like(acc_scr)

    h = h_scr[...]
    a = jnp.dot(h, wa_ref[...], preferred_element_type=F32)
    g = jnp.dot(h, wg_ref[...], preferred_element_type=F32)
    act = (a * jax.nn.sigmoid(a) * g).astype(BF16)
    acc_scr[...] += jnp.dot(act, wo_ref[...], preferred_element_type=F32)

    @pl.when(j == pl.num_programs(1) - 1)
    def _():
        o_ref[...] = x_ref[...] + ga_ref[...] * _rms(acc_scr[...], gpost_ref[...])


def _ffn(x2, mods, l, g_pre, g_post, w_ffn_in, w_ffn_out, *, tm, tiles_per_group, group0):
    m = x2.shape[0]

    def mod_map(chunk):
        return lambda i, j: (l, group0 + i // tiles_per_group, 0, chunk)

    return pl.pallas_call(
        _ffn_kernel,
        grid=(m // tm, FFN_STEPS),
        in_specs=[
            pl.BlockSpec((tm, D_MODEL), lambda i, j: (i, 0)),
            pl.BlockSpec((None, None, 1, D_MODEL), mod_map(3)),
            pl.BlockSpec((None, None, 1, D_MODEL), mod_map(4)),
            pl.BlockSpec((None, None, 1, D_MODEL), mod_map(5)),
            pl.BlockSpec((1, D_MODEL), lambda i, j: (0, 0)),
            pl.BlockSpec((1, D_MODEL), lambda i, j: (0, 0)),
            pl.BlockSpec((D_MODEL, FFN_TH), lambda i, j: (0, j)),
            pl.BlockSpec((D_MODEL, FFN_TH), lambda i, j: (0, FFN_STEPS + j)),
            pl.BlockSpec((FFN_TH, D_MODEL), lambda i, j: (j, 0)),
        ],
        out_specs=pl.BlockSpec((tm, D_MODEL), lambda i, j: (i, 0)),
        out_shape=jax.ShapeDtypeStruct((m, D_MODEL), F32),
        scratch_shapes=[pltpu.VMEM((tm, D_MODEL), BF16), pltpu.VMEM((tm, D_MODEL), F32)],
        compiler_params=_cparams(("arbitrary", "arbitrary")),
        name="ffn",
    )(x2, mods, mods, mods, g_pre.reshape(1, D_MODEL), g_post.reshape(1, D_MODEL),
      w_ffn_in, w_ffn_in, w_ffn_out)


def _rope_tables(n):
    t = jnp.arange(n, dtype=jnp.int32)
    row = (t // GRID_W).astype(F32)
    col = (t % GRID_W).astype(F32)
    inv = ROPE_BASE ** (-jnp.arange(ROPE_FREQS, dtype=F32) / ROPE_FREQS)
    ar = row[:, None] * inv[None, :]
    ac = col[:, None] * inv[None, :]
    cos = jnp.concatenate([jnp.cos(ar), jnp.cos(ar), jnp.cos(ac), jnp.cos(ac)], axis=1)
    sin = jnp.concatenate([-jnp.sin(ar), jnp.sin(ar), -jnp.sin(ac), jnp.sin(ac)], axis=1)
    return cos, sin


def _pick_tile(n, pref):
    t = pref
    while n % t:
        t //= 2
    return t


def kernel(x, c, ctx, c_ctx, w_mod, b_mod, g_pre_mix, g_post_mix, g_pre_ffn, g_post_ffn, w_in, b_in,
           ret_decay, ret_gn, conv_w, conv_b, lru_wa, lru_ba, lru_wx, lru_bx, lru_lambda, w_ret_o,
           w_lru_o, w_out, w_ffn_in, w_ffn_out):
    batch, n, d = x.shape
    n_ctx = ctx.shape[1]
    depth = w_mod.shape[0]
    assert d == D_MODEL and n % RET_CHUNK == 0 and n_ctx % RET_CHUNK == 0

    mod_rows = 8 * ((batch + 1 + 7) // 8)
    c_rows = jnp.zeros((mod_rows, D_MODEL), F32).at[:batch].set(c).at[batch].set(c_ctx)
    mods = _modulation(c_rows, w_mod, b_mod).reshape(depth, mod_rows, 1, 6 * D_MODEL)

    cos_tab, sin_tab = _rope_tables(n)
    cos_ctx = jnp.ones((batch * n_ctx, HEAD_DIM), F32)
    sin_ctx = jnp.zeros((batch * n_ctx, HEAD_DIM), F32)

    tm_lat = _pick_tile(n, 1024)
    tm_ffn = _pick_tile(n, 512)
    tm_ctx = batch * n_ctx
    tt_lat = _pick_tile(n, 1024)
    cpb_lat = _pick_tile(n // RET_CHUNK, 4)

    x2 = x.reshape(batch * n, D_MODEL)
    ctx2 = ctx.reshape(batch * n_ctx, D_MODEL)
    bf = lambda a: a.astype(BF16)

    zeros_s = jnp.zeros((batch, RET_HEADS, HEAD_DIM, HEAD_DIM), F32)
    zeros_h = jnp.zeros((batch, 1, D_MODEL), F32)

    for l in range(depth):
        with_ctx = l < depth - 1
        w_in_l, wa_l, wx_l = bf(w_in[l]), bf(lru_wa[l]), bf(lru_wx[l])
        lru_args = (conv_w[l], conv_b[l], wa_l, lru_ba[l], wx_l, lru_bx[l], lru_lambda[l])

        proj_c = _in_proj(ctx2, mods, l, g_pre_mix[l], w_in_l, b_in[l], cos_ctx, sin_ctx, tm=tm_ctx,
                          tiles_per_group=1, group0=batch, use_rope=False)
        yr_c, s_f, s_b = _retention(proj_c, ret_decay[l], ret_gn[l], zeros_s, zeros_s,
                                    batch=batch, n=n_ctx, cpb=n_ctx // RET_CHUNK)
        yl_c, h_f, h_b = _rglru(proj_c, *lru_args, zeros_h, zeros_h, batch=batch, n=n_ctx, tt=n_ctx)

        proj = _in_proj(x2, mods, l, g_pre_mix[l], w_in_l, b_in[l], cos_tab, sin_tab, tm=tm_lat,
                        tiles_per_group=n // tm_lat, group0=0, use_rope=True)
        y_r, _, _ = _retention(proj, ret_decay[l], ret_gn[l], s_f, s_b, batch=batch, n=n, cpb=cpb_lat)
        y_l, _, _ = _rglru(proj, *lru_args, h_f, h_b, batch=batch, n=n, tt=tt_lat)

        w_ro, w_lo, w_o = bf(w_ret_o[l]), bf(w_lru_o[l]), bf(w_out[l])
        w_fi, w_fo = bf(w_ffn_in[l]), bf(w_ffn_out[l])
        x2 = _merge(y_r, y_l, proj, x2, mods, l, g_post_mix[l], w_ro, w_lo, w_o,
                    tiles_per_group=n // MERGE_TM, group0=0)
        x2 = _ffn(x2, mods, l, g_pre_ffn[l], g_post_ffn[l], w_fi, w_fo, tm=tm_ffn,
                  tiles_per_group=n // tm_ffn, group0=0)
        if with_ctx:
            ctx2 = _merge(yr_c, yl_c, proj_c, ctx2, mods, l, g_post_mix[l], w_ro, w_lo, w_o,
                          tiles_per_group=batch * n_ctx // MERGE_TM, group0=batch)
            ctx2 = _ffn(ctx2, mods, l, g_pre_ffn[l], g_post_ffn[l], w_fi, w_fo, tm=tm_ctx,
                        tiles_per_group=1, group0=batch)
    return x2.reshape(batch, n, D_MODEL)
```

```python
import functools
import math

import jax
import jax.numpy as jnp
from jax import lax
from jax.experimental import pallas as pl
from jax.experimental.pallas import tpu as pltpu

F32 = jnp.float32
BF16 = jnp.bfloat16

D_MODEL = 2048
GRID_W = 64
RET_HEADS = 8
HEAD_DIM = D_MODEL // RET_HEADS
ROPE_HALF = HEAD_DIM // 2
ROPE_FREQS = ROPE_HALF // 2
ROPE_BASE = 10000.0
LRU_BLOCKS = 8
LRU_BLOCK_DIM = D_MODEL // LRU_BLOCKS
LRU_C = 8.0
FFN_HIDDEN = ((8 * D_MODEL // 3 + 255) // 256) * 256
NORM_EPS = 1e-6
N_SEG = 8
IN_COLS = N_SEG * D_MODEL
K_SCALE = HEAD_DIM ** -0.5

LANES = 128
SUB = 8
RET_CHUNK = 256
VMEM_LIMIT = 58 * 1024 * 1024


def _cparams(sem):
    return pltpu.CompilerParams(dimension_semantics=sem, vmem_limit_bytes=VMEM_LIMIT)


def _rms(x, g):
    ms = jnp.mean(x * x, axis=-1, keepdims=True)
    return x * lax.rsqrt(ms + NORM_EPS) * g


def _gelu_tanh(x):
    c = 0.7978845608028654
    return 0.5 * x * (1.0 + jnp.tanh(c * (x + 0.044715 * (x * x * x))))


def _sigmoid(x):
    return 0.5 * jnp.tanh(0.5 * x) + 0.5


def _softplus(x):
    return jnp.maximum(x, 0.0) + jnp.log1p(jnp.exp(-jnp.abs(x)))


MOD_TN = 1024


def _mod_kernel(c_ref, w_ref, b_ref, o_ref):
    c = c_ref[...]
    s = (c * jax.nn.sigmoid(c)).astype(BF16)
    o_ref[...] = jnp.dot(s, w_ref[...].astype(BF16), preferred_element_type=F32) + b_ref[...]


def _modulation(c_rows, w_mod, b_mod):
    depth = w_mod.shape[0]
    rows = c_rows.shape[0]
    ncol = w_mod.shape[2]
    return pl.pallas_call(
        _mod_kernel,
        grid=(depth, ncol // MOD_TN),
        in_specs=[
            pl.BlockSpec((rows, D_MODEL), lambda l, j: (0, 0)),
            pl.BlockSpec((None, D_MODEL, MOD_TN), lambda l, j: (l, 0, j)),
            pl.BlockSpec((None, 1, MOD_TN), lambda l, j: (l, 0, j)),
        ],
        out_specs=pl.BlockSpec((None, rows, MOD_TN), lambda l, j: (l, 0, j)),
        out_shape=jax.ShapeDtypeStruct((depth, rows, ncol), F32),
        compiler_params=_cparams(("arbitrary", "arbitrary")),
        name="modulation",
    )(c_rows, w_mod, b_mod.reshape(depth, 1, ncol))


INPROJ_TN = 1024
STEPS_PER_SEG = D_MODEL // INPROJ_TN


def _inproj_kernel(x_ref, sh_ref, sc_ref, g_ref, w_ref, b_ref, cos_ref, sin_ref, o_ref, h_scr,
                   *, use_rope):
    j = pl.program_id(1)

    @pl.when(j == 0)
    def _():
        y = _rms(x_ref[...], g_ref[...])
        h_scr[...] = (y * (1.0 + sc_ref[...]) + sh_ref[...]).astype(BF16)

    seg = j // STEPS_PER_SEG

    def proj():
        return jnp.dot(h_scr[...], w_ref[...], preferred_element_type=F32) + b_ref[...]

    def rope(t):
        cos = cos_ref[...]
        sin = sin_ref[...]
        outs = []
        for hh in range(INPROJ_TN // HEAD_DIM):
            for part in range(HEAD_DIM // ROPE_HALF):
                lo = hh * HEAD_DIM + part * ROPE_HALF
                tt = t[:, lo:lo + ROPE_HALF]
                cs = cos[:, part * ROPE_HALF:(part + 1) * ROPE_HALF]
                sn = sin[:, part * ROPE_HALF:(part + 1) * ROPE_HALF]
                outs.append(tt * cs + pltpu.roll(tt, ROPE_FREQS, 1) * sn)
        return jnp.concatenate(outs, axis=1)

    @pl.when(seg == 0)
    def _():
        t = proj()
        o_ref[...] = (rope(t) if use_rope else t).astype(BF16)

    @pl.when(seg == 1)
    def _():
        t = proj()
        o_ref[...] = ((rope(t) if use_rope else t) * K_SCALE).astype(BF16)

    @pl.when((seg == 2) | (seg == 4))
    def _():
        o_ref[...] = proj().astype(BF16)

    @pl.when(seg == 3)
    def _():
        t = proj()
        o_ref[...] = (t * _sigmoid(t)).astype(BF16)

    @pl.when(seg == 5)
    def _():
        o_ref[...] = _gelu_tanh(proj()).astype(BF16)

    @pl.when(seg >= 6)
    def _():
        o_ref[...] = _sigmoid(proj()).astype(BF16)


def _in_proj(x2, mods, l, g_pre, w_in, b_in, cos_tab, sin_tab, *, tm, tiles_per_group, group0,
             use_rope):
    m = x2.shape[0]
    nt = m // tm
    tab_tiles = cos_tab.shape[0] // tm

    def mod_map(chunk):
        return lambda i, j: (l, group0 + i // tiles_per_group, 0, chunk)

    return pl.pallas_call(
        functools.partial(_inproj_kernel, use_rope=use_rope),
        grid=(nt, IN_COLS // INPROJ_TN),
        in_specs=[
            pl.BlockSpec((tm, D_MODEL), lambda i, j: (i, 0)),
            pl.BlockSpec((None, None, 1, D_MODEL), mod_map(0)),
            pl.BlockSpec((None, None, 1, D_MODEL), mod_map(1)),
            pl.BlockSpec((1, D_MODEL), lambda i, j: (0, 0)),
            pl.BlockSpec((None, D_MODEL, INPROJ_TN), lambda i, j: (l, 0, j)),
            pl.BlockSpec((1, INPROJ_TN), lambda i, j: (0, j)),
            pl.BlockSpec((tm, HEAD_DIM), lambda i, j: (i % tab_tiles, 0)),
            pl.BlockSpec((tm, HEAD_DIM), lambda i, j: (i % tab_tiles, 0)),
        ],
        out_specs=pl.BlockSpec((tm, INPROJ_TN), lambda i, j: (i, j)),
        out_shape=jax.ShapeDtypeStruct((m, IN_COLS), BF16),
        scratch_shapes=[pltpu.VMEM((tm, D_MODEL), BF16)],
        compiler_params=_cparams(("arbitrary", "arbitrary")),
        name="in_proj",
    )(x2, mods, mods, g_pre.reshape(1, D_MODEL), w_in, b_in.reshape(1, IN_COLS), cos_tab, sin_tab)


def _log_sigmoid(x):
    return jnp.minimum(x, 0.0) - jnp.log1p(jnp.exp(-jnp.abs(x)))


_TN_DIMS = (((0,), (0,)), ((), ()))
_NT_DIMS = (((1,), (1,)), ((), ()))


def _ret_bwd_kernel(dec_ref, k_ref, v_ref, s0_ref, sb_ref, sfin_ref, s_scr, dk_scr, ds_scr,
                    *, cpb):
    h = pl.program_id(1)
    j = pl.program_id(2)
    c = RET_CHUNK

    @pl.when(j == 0)
    def _():
        s_scr[...] = s0_ref[...]
        lg = _log_sigmoid(jnp.full((c, HEAD_DIM), dec_ref[1, h], F32))
        row = lax.broadcasted_iota(jnp.int32, (c, HEAD_DIM), 0).astype(F32)
        dk_scr[...] = jnp.exp(row * lg)
        ds_scr[...] = jnp.exp(float(c) * _log_sigmoid(jnp.full((SUB, HEAD_DIM), dec_ref[1, h], F32)))

    for cc in range(cpb - 1, -1, -1):
        sb_ref[cc] = s_scr[...].astype(BF16)
        k = k_ref[cc * c:(cc + 1) * c, :]
        v = v_ref[cc * c:(cc + 1) * c, :]
        vb = (v.astype(F32) * dk_scr[...]).astype(BF16)
        kv = lax.dot_general(k, vb, _TN_DIMS, preferred_element_type=F32)
        s_scr[...] = s_scr[...] * ds_scr[0:1, :] + kv

    @pl.when(j == pl.num_programs(2) - 1)
    def _():
        sfin_ref[...] = s_scr[...]


def _ret_fwd_kernel(dec_ref, q_ref, k_ref, v_ref, og_ref, sb_ref, s0_ref, gn_ref, y_ref, sfin_ref,
                    s_scr, mask_scr, dqf_scr, dqb_scr, dk_scr, ds_scr, *, cpb):
    h = pl.program_id(1)
    j = pl.program_id(2)
    c = RET_CHUNK

    @pl.when(j == 0)
    def _():
        s_scr[...] = s0_ref[...]
        lgf = _log_sigmoid(jnp.full((c, c), dec_ref[0, h], F32))
        lgb = _log_sigmoid(jnp.full((c, c), dec_ref[1, h], F32))
        ii = lax.broadcasted_iota(jnp.int32, (c, c), 0)
        jj = lax.broadcasted_iota(jnp.int32, (c, c), 1)
        d = (ii - jj).astype(F32)
        fwd = jnp.where(d >= 0.0, jnp.exp(jnp.maximum(d, 0.0) * lgf), 0.0)
        bwd = jnp.where(d <= 0.0, jnp.exp(jnp.maximum(-d, 0.0) * lgb), 0.0)
        mask_scr[...] = fwd + bwd
        row = ii.astype(F32)
        dqf_scr[...] = jnp.exp((row + 1.0) * lgf)
        dqb_scr[...] = jnp.exp((float(c) - row) * lgb)
        dk_scr[...] = jnp.exp((float(c) - 1.0 - row) * lgf)
        ds_scr[...] = jnp.exp(float(c) * _log_sigmoid(jnp.full((SUB, HEAD_DIM), dec_ref[0, h], F32)))

    for cc in range(cpb):
        sl = slice(cc * c, (cc + 1) * c)
        q = q_ref[sl, :]
        k = k_ref[sl, :]
        v = v_ref[sl, :]
        att = lax.dot_general(q, k, _NT_DIMS, preferred_element_type=F32)
        p = (att * mask_scr[...]).astype(BF16)
        o = jnp.dot(p, v, preferred_element_type=F32)
        o = o + dqf_scr[...] * jnp.dot(q, s_scr[...].astype(BF16), preferred_element_type=F32)
        o = o + dqb_scr[...] * jnp.dot(q, sb_ref[cc], preferred_element_type=F32)
        vf = (v.astype(F32) * dk_scr[...]).astype(BF16)
        kv = lax.dot_general(k, vf, _TN_DIMS, preferred_element_type=F32)
        s_scr[...] = s_scr[...] * ds_scr[0:1, :] + kv
        mu = jnp.mean(o, axis=-1, keepdims=True)
        oc = o - mu
        var = jnp.mean(oc * oc, axis=-1, keepdims=True)
        y = oc * lax.rsqrt(var + NORM_EPS) * gn_ref[...]
        y_ref[sl, :] = (y * og_ref[sl, :].astype(F32)).astype(BF16)

    @pl.when(j == pl.num_programs(2) - 1)
    def _():
        sfin_ref[...] = s_scr[...]


def _retention(proj, ret_decay, ret_gn, s0_f, s0_b, *, batch, n, cpb):
    c = RET_CHUNK
    rows = cpb * c
    nb = n // rows
    hd = HEAD_DIM
    state_spec = pl.BlockSpec((None, None, hd, hd), lambda b, h, j: (b, h, 0, 0))
    smem = pl.BlockSpec(memory_space=pltpu.SMEM)

    def col(seg, rev):
        if rev:
            return pl.BlockSpec((rows, hd), lambda b, h, j: (b * nb + nb - 1 - j, seg * RET_HEADS + h))
        return pl.BlockSpec((rows, hd), lambda b, h, j: (b * nb + j, seg * RET_HEADS + h))

    sb, sfin_b = pl.pallas_call(
        functools.partial(_ret_bwd_kernel, cpb=cpb),
        grid=(batch, RET_HEADS, nb),
        in_specs=[smem, col(1, True), col(2, True), state_spec],
        out_specs=[
            pl.BlockSpec((None, None, cpb, hd, hd), lambda b, h, j: (b, h, nb - 1 - j, 0, 0)),
            state_spec,
        ],
        out_shape=[
            jax.ShapeDtypeStruct((batch, RET_HEADS, nb * cpb, hd, hd), BF16),
            jax.ShapeDtypeStruct((batch, RET_HEADS, hd, hd), F32),
        ],
        scratch_shapes=[pltpu.VMEM((hd, hd), F32), pltpu.VMEM((c, hd), F32), pltpu.VMEM((SUB, hd), F32)],
        compiler_params=_cparams(("arbitrary", "arbitrary", "arbitrary")),
        name="ret_bwd",
    )(ret_decay, proj, proj, s0_b)

    y, sfin_f = pl.pallas_call(
        functools.partial(_ret_fwd_kernel, cpb=cpb),
        grid=(batch, RET_HEADS, nb),
        in_specs=[
            smem, col(0, False), col(1, False), col(2, False), col(3, False),
            pl.BlockSpec((None, None, cpb, hd, hd), lambda b, h, j: (b, h, j, 0, 0)),
            state_spec,
            pl.BlockSpec((1, hd), lambda b, h, j: (0, h)),
        ],
        out_specs=[
            pl.BlockSpec((rows, hd), lambda b, h, j: (b * nb + j, h)),
            state_spec,
        ],
        out_shape=[
            jax.ShapeDtypeStruct((batch * n, D_MODEL), BF16),
            jax.ShapeDtypeStruct((batch, RET_HEADS, hd, hd), F32),
        ],
        scratch_shapes=[
            pltpu.VMEM((hd, hd), F32), pltpu.VMEM((c, c), F32), pltpu.VMEM((c, hd), F32),
            pltpu.VMEM((c, hd), F32), pltpu.VMEM((c, hd), F32), pltpu.VMEM((SUB, hd), F32),
        ],
        compiler_params=_cparams(("arbitrary", "arbitrary", "arbitrary")),
        name="ret_fwd",
    )(ret_decay, proj, proj, proj, proj, sb, s0_f, ret_gn.reshape(1, D_MODEL))
    return y, sfin_f, sfin_b


HALO = 16
LOG2E = math.log2(math.e)
TINY = 1e-30


def _row_scan(a, u, carry, reverse):
    rows, w = a.shape
    nv = rows // SUB
    a3 = a.reshape(nv, SUB, w)
    u3 = u.reshape(nv, SUB, w)
    sub = lax.broadcasted_iota(jnp.int32, (nv, SUB, w), 1)
    for d in (1, 2, 4):
        valid = sub < SUB - d if reverse else sub >= d
        shift = SUB - d if reverse else d
        a_sh = jnp.where(valid, pltpu.roll(a3, shift, 1), 1.0)
        u_sh = jnp.where(valid, pltpu.roll(u3, shift, 1), 0.0)
        u3 = a3 * u_sh + u3
        a3 = a3 * a_sh
    edge = 0 if reverse else SUB - 1
    out = [None] * nv
    for v in (range(nv - 1, -1, -1) if reverse else range(nv)):
        e = u3[v] + a3[v] * carry
        out[v] = e
        carry = e[edge:edge + 1, :]
    return jnp.concatenate(out, axis=0)


def _lru_kernel(*refs, reverse, tt):
    if reverse:
        (xp_ref, x_ref, xn_ref, cw_ref, cb_ref, wa_ref, ba_ref, wx_ref, bx_ref, lam_ref, h0_ref,
         out_ref, hfin_ref, carry_scr, xs_scr) = refs
    else:
        (xp_ref, x_ref, xn_ref, cw_ref, cb_ref, wa_ref, ba_ref, wx_ref, bx_ref, lam_ref, h0_ref,
         hb_ref, gate_ref, out_ref, hfin_ref, carry_scr, xs_scr, os_scr) = refs
    i = pl.program_id(2)
    nt = pl.num_programs(2)
    ti = nt - 1 - i if reverse else i
    w = LRU_BLOCK_DIM
    g = tt // SUB
    slabs = w // LANES

    @pl.when(i == 0)
    def _():
        carry_scr[...] = h0_ref[...]

    x = x_ref[...].astype(F32)
    prev = jnp.where(ti == 0, 0.0, xp_ref[...].astype(F32)[HALO - SUB:HALO, :])
    nxt = jnp.where(ti == nt - 1, 0.0, xn_ref[...].astype(F32)[0:SUB, :])
    for s in range(slabs):
        ls = slice(s * LANES, (s + 1) * LANES)
        xs_scr[s, 0:SUB, :] = prev[:, ls]
        xs_scr[s, SUB:SUB + tt, :] = x[:, ls]
        xs_scr[s, SUB + tt:2 * SUB + tt, :] = nxt[:, ls]

    def phase(k):
        return jnp.concatenate(
            [xs_scr[s, pl.ds(SUB + k, g, stride=SUB), :] for s in range(slabs)], axis=1)

    ph = [phase(k) for k in range(-2, SUB + 1)]
    cw = cw_ref[...]
    cb = cb_ref[...]
    xc = jnp.concatenate(
        [cw[0:1, :] * ph[k] + cw[1:2, :] * ph[k + 1] + cw[2:3, :] * ph[k + 2]
         + cw[3:4, :] * ph[k + 3] + cb for k in range(SUB)], axis=0)

    xb = xc.astype(BF16)
    tr = jnp.tanh(jnp.dot(xb, wa_ref[...], preferred_element_type=F32) + 0.5 * ba_ref[...])
    tg = jnp.tanh(jnp.dot(xb, wx_ref[...], preferred_element_type=F32) + 0.5 * bx_ref[...])
    k2h = (-0.5 * LRU_C * LOG2E) * _softplus(-lam_ref[...])
    a = jnp.exp2(tr * k2h + k2h)
    y = 1.0 - a * a
    root = y * lax.rsqrt(jnp.maximum(y, TINY))
    u = root * ((0.5 * tg + 0.5) * xc)

    order = range(SUB - 1, -1, -1) if reverse else range(SUB)
    hloc = [None] * SUB
    ploc = [None] * SUB
    hp = pp = None
    for k in order:
        ak = a[k * g:(k + 1) * g, :]
        uk = u[k * g:(k + 1) * g, :]
        hp = uk if hp is None else ak * hp + uk
        pp = ak if pp is None else ak * pp
        hloc[k] = hp
        ploc[k] = pp

    carry = carry_scr[...]
    e = _row_scan(pp, hp, carry, reverse)
    rowi = lax.broadcasted_iota(jnp.int32, (g, w), 0)
    if reverse:
        cin = jnp.where(rowi == g - 1, carry, pltpu.roll(e, g - 1, 0))
        new_carry = e[0:1, :]
    else:
        cin = jnp.where(rowi == 0, carry, pltpu.roll(e, 1, 0))
        new_carry = e[g - 1:g, :]
    carry_scr[...] = new_carry

    if reverse:
        for k in range(SUB):
            out_ref[k * g:(k + 1) * g, :] = (hloc[k] + ploc[k] * cin).astype(BF16)
    else:
        for k in range(SUB):
            hk = hloc[k] + ploc[k] * cin + hb_ref[k * g:(k + 1) * g, :].astype(F32)
            for s in range(slabs):
                os_scr[s, pl.ds(k, g, stride=SUB), :] = hk[:, s * LANES:(s + 1) * LANES]
        tok = jnp.concatenate([os_scr[s] for s in range(slabs)], axis=1)
        out_ref[...] = (tok * gate_ref[...].astype(F32)).astype(BF16)

    @pl.when(i == nt - 1)
    def _():
        hfin_ref[...] = new_carry


def _rglru(proj, l, conv_w, conv_b, lru_wa, lru_ba, lru_wx, lru_bx, lru_lambda, h0_f, h0_b,
           *, batch, n, tt):
    w = LRU_BLOCK_DIM
    nt = n // tt
    hpt = tt // HALO
    nh = n // HALO
    xseg = 4 * LRU_BLOCKS
    gseg = 5 * LRU_BLOCKS
    slabs = w // LANES

    def specs(rev):
        d = 1 if rev else 0

        def t_of(i):
            return nt - 1 - i if rev else i

        def vec(arr_rows):
            return pl.BlockSpec((arr_rows, w), lambda b, cb, i: (0, cb))

        wspec = pl.BlockSpec((None, None, None, w, w), lambda b, cb, i: (l, d, cb, 0, 0))
        return [
            pl.BlockSpec((HALO, w), lambda b, cb, i: (b * nh + jnp.maximum(t_of(i) * hpt - 1, 0), xseg + cb)),
            pl.BlockSpec((tt, w), lambda b, cb, i: (b * nt + t_of(i), xseg + cb)),
            pl.BlockSpec((HALO, w), lambda b, cb, i: (b * nh + jnp.minimum((t_of(i) + 1) * hpt, nh - 1), xseg + cb)),
            vec(4), vec(1), wspec, vec(1), wspec, vec(1), vec(1),
            pl.BlockSpec((None, 1, w), lambda b, cb, i: (b, 0, cb)),
        ]

    stage = pltpu.VMEM((slabs, tt + 2 * SUB, LANES), F32)
    state_spec = pl.BlockSpec((None, 1, w), lambda b, cb, i: (b, 0, cb))
    state_shape = jax.ShapeDtypeStruct((batch, 1, D_MODEL), F32)
    grid = (batch, LRU_BLOCKS, nt)
    params = _cparams(("arbitrary", "arbitrary", "arbitrary"))

    def small(d):
        return (conv_w, conv_b.reshape(1, -1), lru_wa, lru_ba[d].reshape(1, -1), lru_wx,
                lru_bx[d].reshape(1, -1), lru_lambda[d].reshape(1, -1))

    hb, hfin_b = pl.pallas_call(
        functools.partial(_lru_kernel, reverse=True, tt=tt),
        grid=grid,
        in_specs=specs(True),
        out_specs=[pl.BlockSpec((tt, w), lambda b, cb, i: (b * nt + nt - 1 - i, cb)), state_spec],
        out_shape=[jax.ShapeDtypeStruct((batch * n, D_MODEL), BF16), state_shape],
        scratch_shapes=[pltpu.VMEM((1, w), F32), stage],
        compiler_params=params,
        name="lru_bwd",
    )(proj, proj, proj, *small(1), h0_b)

    y, hfin_f = pl.pallas_call(
        functools.partial(_lru_kernel, reverse=False, tt=tt),
        grid=grid,
        in_specs=specs(False) + [
            pl.BlockSpec((tt, w), lambda b, cb, i: (b * nt + i, cb)),
            pl.BlockSpec((tt, w), lambda b, cb, i: (b * nt + i, gseg + cb)),
        ],
        out_specs=[pl.BlockSpec((tt, w), lambda b, cb, i: (b * nt + i, cb)), state_spec],
        out_shape=[jax.ShapeDtypeStruct((batch * n, D_MODEL), BF16), state_shape],
        scratch_shapes=[pltpu.VMEM((1, w), F32), stage, pltpu.VMEM((slabs, tt, LANES), F32)],
        compiler_params=params,
        name="lru_fwd",
    )(proj, proj, proj, *small(0), h0_f, hb, proj)
    return y, hfin_f, hfin_b


MERGE_TM = 256


def _merge_kernel(yr_ref, yl_ref, sa_ref, sb_ref, x_ref, ga_ref, g_ref, wr_ref, wl_ref, wo_ref,
                  o_ref):
    pr = jnp.dot(yr_ref[...], wr_ref[...], preferred_element_type=F32)
    plru = jnp.dot(yl_ref[...], wl_ref[...], preferred_element_type=F32)
    y = sa_ref[...].astype(F32) * pr + sb_ref[...].astype(F32) * plru
    z = jnp.dot(y.astype(BF16), wo_ref[...], preferred_element_type=F32)
    o_ref[...] = x_ref[...] + ga_ref[...] * _rms(z, g_ref[...])


def _merge(y_r, y_l, proj, x2, mods, l, g_post, w_ret_o, w_lru_o, w_out, *, tiles_per_group, group0):
    m = x2.shape[0]
    tm = MERGE_TM
    row = lambda i: (i, 0)
    wspec = pl.BlockSpec((None, D_MODEL, D_MODEL), lambda i: (l, 0, 0), pipeline_mode=pl.Buffered(1))
    return pl.pallas_call(
        _merge_kernel,
        grid=(m // tm,),
        in_specs=[
            pl.BlockSpec((tm, D_MODEL), row),
            pl.BlockSpec((tm, D_MODEL), row),
            pl.BlockSpec((tm, D_MODEL), lambda i: (i, 6)),
            pl.BlockSpec((tm, D_MODEL), lambda i: (i, 7)),
            pl.BlockSpec((tm, D_MODEL), row),
            pl.BlockSpec((None, None, 1, D_MODEL), lambda i: (l, group0 + i // tiles_per_group, 0, 2)),
            pl.BlockSpec((1, D_MODEL), lambda i: (0, 0)),
            wspec, wspec, wspec,
        ],
        out_specs=pl.BlockSpec((tm, D_MODEL), row),
        out_shape=jax.ShapeDtypeStruct((m, D_MODEL), F32),
        compiler_params=_cparams(("arbitrary",)),
        name="merge",
    )(y_r, y_l, proj, proj, x2, mods, g_post.reshape(1, D_MODEL), w_ret_o, w_lru_o, w_out)


FFN_TH = 512
FFN_STEPS = FFN_HIDDEN // FFN_TH


def _ffn_kernel(x_ref, sh_ref, sc_ref, ga_ref, gpre_ref, gpost_ref, wa_ref, wg_ref, wo_ref, o_ref,
                h_scr):
    j = pl.program_id(1)

    @pl.when(j == 0)
    def _():
        y = _rms(x_ref[...], gpre_ref[...])
        h_scr[...] = (y * (1.0 + sc_ref[...]) + sh_ref[...]).astype(BF16)
        o_ref[...] = jnp.zeros_like(o_ref)

    h = h_scr[...]
    a = jnp.dot(h, wa_ref[...], preferred_element_type=F32)
    g = jnp.dot(h, wg_ref[...], preferred_element_type=F32)
    act = (a * _sigmoid(a) * g).astype(BF16)
    o_ref[...] += jnp.dot(act, wo_ref[...], preferred_element_type=F32)

    @pl.when(j == pl.num_programs(1) - 1)
    def _():
        o_ref[...] = x_ref[...] + ga_ref[...] * _rms(o_ref[...], gpost_ref[...])


def _ffn(x2, mods, l, g_pre, g_post, w_ffn_in, w_ffn_out, *, tm, tiles_per_group, group0):
    m = x2.shape[0]

    def mod_map(chunk):
        return lambda i, j: (l, group0 + i // tiles_per_group, 0, chunk)

    return pl.pallas_call(
        _ffn_kernel,
        grid=(m // tm, FFN_STEPS),
        in_specs=[
            pl.BlockSpec((tm, D_MODEL), lambda i, j: (i, 0)),
            pl.BlockSpec((None, None, 1, D_MODEL), mod_map(3)),
            pl.BlockSpec((None, None, 1, D_MODEL), mod_map(4)),
            pl.BlockSpec((None, None, 1, D_MODEL), mod_map(5)),
            pl.BlockSpec((1, D_MODEL), lambda i, j: (0, 0)),
            pl.BlockSpec((1, D_MODEL), lambda i, j: (0, 0)),
            pl.BlockSpec((None, D_MODEL, FFN_TH), lambda i, j: (l, 0, j)),
            pl.BlockSpec((None, D_MODEL, FFN_TH), lambda i, j: (l, 0, FFN_STEPS + j)),
            pl.BlockSpec((None, FFN_TH, D_MODEL), lambda i, j: (l, j, 0)),
        ],
        out_specs=pl.BlockSpec((tm, D_MODEL), lambda i, j: (i, 0)),
        out_shape=jax.ShapeDtypeStruct((m, D_MODEL), F32),
        scratch_shapes=[pltpu.VMEM((tm, D_MODEL), BF16)],
        compiler_params=_cparams(("arbitrary", "arbitrary")),
        name="ffn",
    )(x2, mods, mods, mods, g_pre.reshape(1, D_MODEL), g_post.reshape(1, D_MODEL),
      w_ffn_in, w_ffn_in, w_ffn_out)


def _rope_tables(n):
    t = jnp.arange(n, dtype=jnp.int32)
    row = (t // GRID_W).astype(F32)
    col = (t % GRID_W).astype(F32)
    inv = ROPE_BASE ** (-jnp.arange(ROPE_FREQS, dtype=F32) / ROPE_FREQS)
    ar = row[:, None] * inv[None, :]
    ac = col[:, None] * inv[None, :]
    cos = jnp.concatenate([jnp.cos(ar), jnp.cos(ar), jnp.cos(ac), jnp.cos(ac)], axis=1)
    sin = jnp.concatenate([-jnp.sin(ar), jnp.sin(ar), -jnp.sin(ac), jnp.sin(ac)], axis=1)
    return cos, sin


def _pick_tile(n, pref):
    t = pref
    while n % t:
        t //= 2
    return t


def kernel(x, c, ctx, c_ctx, w_mod, b_mod, g_pre_mix, g_post_mix, g_pre_ffn, g_post_ffn, w_in, b_in,
           ret_decay, ret_gn, conv_w, conv_b, lru_wa, lru_ba, lru_wx, lru_bx, lru_lambda, w_ret_o,
           w_lru_o, w_out, w_ffn_in, w_ffn_out):
    batch, n, d = x.shape
    n_ctx = ctx.shape[1]
    depth = w_mod.shape[0]
    assert d == D_MODEL and n % RET_CHUNK == 0 and n_ctx % RET_CHUNK == 0
    assert RET_CHUNK == HEAD_DIM

    mod_rows = SUB * ((batch + 1 + SUB - 1) // SUB)
    c_rows = jnp.zeros((mod_rows, D_MODEL), F32).at[:batch].set(c).at[batch].set(c_ctx)
    mods = _modulation(c_rows, w_mod, b_mod).reshape(depth, mod_rows, 1, 6 * D_MODEL)

    cos_tab, sin_tab = _rope_tables(n)
    cos_ctx = jnp.ones((batch * n_ctx, HEAD_DIM), F32)
    sin_ctx = jnp.zeros((batch * n_ctx, HEAD_DIM), F32)

    tm_lat = _pick_tile(n, 1024)
    tm_ffn = _pick_tile(n, 512)
    tm_ctx = batch * n_ctx
    tt_lat = _pick_tile(n, 2048)
    cpb_lat = _pick_tile(n // RET_CHUNK, 8)

    x2 = x.reshape(batch * n, D_MODEL)
    ctx2 = ctx.reshape(batch * n_ctx, D_MODEL)

    w_in = w_in.astype(BF16)
    lru_wa, lru_wx = (0.5 * lru_wa).astype(BF16), (0.5 * lru_wx).astype(BF16)
    w_ret_o, w_lru_o, w_out = w_ret_o.astype(BF16), w_lru_o.astype(BF16), w_out.astype(BF16)
    w_ffn_in, w_ffn_out = w_ffn_in.astype(BF16), w_ffn_out.astype(BF16)

    zeros_s = jnp.zeros((batch, RET_HEADS, HEAD_DIM, HEAD_DIM), F32)
    zeros_h = jnp.zeros((batch, 1, D_MODEL), F32)

    for l in range(depth):
        with_ctx = l < depth - 1
        lru_args = (conv_w[l], conv_b[l], lru_wa, lru_ba[l], lru_wx, lru_bx[l], lru_lambda[l])

        proj_c = _in_proj(ctx2, mods, l, g_pre_mix[l], w_in, b_in[l], cos_ctx, sin_ctx, tm=tm_ctx,
                          tiles_per_group=1, group0=batch, use_rope=False)
        yr_c, s_f, s_b = _retention(proj_c, ret_decay[l], ret_gn[l], zeros_s, zeros_s,
                                    batch=batch, n=n_ctx, cpb=n_ctx // RET_CHUNK)
        yl_c, h_f, h_b = _rglru(proj_c, l, *lru_args, zeros_h, zeros_h, batch=batch, n=n_ctx, tt=n_ctx)

        proj = _in_proj(x2, mods, l, g_pre_mix[l], w_in, b_in[l], cos_tab, sin_tab, tm=tm_lat,
                        tiles_per_group=n // tm_lat, group0=0, use_rope=True)
        y_r, _, _ = _retention(proj, ret_decay[l], ret_gn[l], s_f, s_b, batch=batch, n=n, cpb=cpb_lat)
        y_l, _, _ = _rglru(proj, l, *lru_args, h_f, h_b, batch=batch, n=n, tt=tt_lat)

        x2 = _merge(y_r, y_l, proj, x2, mods, l, g_post_mix[l], w_ret_o, w_lru_o, w_out,
                    tiles_per_group=n // MERGE_TM, group0=0)
        x2 = _ffn(x2, mods, l, g_pre_ffn[l], g_post_ffn[l], w_ffn_in, w_ffn_out, tm=tm_ffn,
                  tiles_per_group=n // tm_ffn, group0=0)
        if with_ctx:
            ctx2 = _merge(yr_c, yl_c, proj_c, ctx2, mods, l, g_post_mix[l], w_ret_o, w_lru_o, w_out,
                          tiles_per_group=batch * n_ctx // MERGE_TM, group0=batch)
            ctx2 = _ffn(ctx2, mods, l, g_pre_ffn[l], g_post_ffn[l], w_ffn_in, w_ffn_out, tm=tm_ctx,
                        tiles_per_group=1, group0=batch)
    return x2.reshape(batch, n, D_MODEL)
```

```python
import functools
import math

import jax
import jax.numpy as jnp
from jax import lax
from jax.experimental import pallas as pl
from jax.experimental.pallas import tpu as pltpu

F32 = jnp.float32
BF16 = jnp.bfloat16

D_MODEL = 2048
GRID_W = 64
RET_HEADS = 8
HEAD_DIM = D_MODEL // RET_HEADS
ROPE_HALF = HEAD_DIM // 2
ROPE_FREQS = ROPE_HALF // 2
ROPE_BASE = 10000.0
LRU_BLOCKS = 8
LRU_BLOCK_DIM = D_MODEL // LRU_BLOCKS
LRU_C = 8.0
FFN_HIDDEN = ((8 * D_MODEL // 3 + 255) // 256) * 256
NORM_EPS = 1e-6
N_SEG = 8
IN_COLS = N_SEG * D_MODEL
K_SCALE = HEAD_DIM ** -0.5

LANES = 128
SUB = 8
RET_CHUNK = 256
VMEM_LIMIT = 58 * 1024 * 1024


def _cparams(sem):
    return pltpu.CompilerParams(dimension_semantics=sem, vmem_limit_bytes=VMEM_LIMIT)


def _rms(x, g):
    ms = jnp.mean(x * x, axis=-1, keepdims=True)
    return x * lax.rsqrt(ms + NORM_EPS) * g


def _gelu_tanh(x):
    c = 0.7978845608028654
    return 0.5 * x * (1.0 + jnp.tanh(c * (x + 0.044715 * (x * x * x))))


def _sigmoid(x):
    return 0.5 * jnp.tanh(0.5 * x) + 0.5


def _softplus(x):
    return jnp.maximum(x, 0.0) + jnp.log1p(jnp.exp(-jnp.abs(x)))


MOD_TN = 1024


def _mod_kernel(c_ref, w_ref, b_ref, o_ref):
    c = c_ref[...]
    s = (c * jax.nn.sigmoid(c)).astype(BF16)
    o_ref[...] = jnp.dot(s, w_ref[...].astype(BF16), preferred_element_type=F32) + b_ref[...]


def _modulation(c_rows, w_mod, b_mod):
    depth = w_mod.shape[0]
    rows = c_rows.shape[0]
    ncol = w_mod.shape[2]
    return pl.pallas_call(
        _mod_kernel,
        grid=(depth, ncol // MOD_TN),
        in_specs=[
            pl.BlockSpec((rows, D_MODEL), lambda l, j: (0, 0)),
            pl.BlockSpec((None, D_MODEL, MOD_TN), lambda l, j: (l, 0, j)),
            pl.BlockSpec((None, 1, MOD_TN), lambda l, j: (l, 0, j)),
        ],
        out_specs=pl.BlockSpec((None, rows, MOD_TN), lambda l, j: (l, 0, j)),
        out_shape=jax.ShapeDtypeStruct((depth, rows, ncol), F32),
        compiler_params=_cparams(("arbitrary", "arbitrary")),
        name="modulation",
    )(c_rows, w_mod, b_mod.reshape(depth, 1, ncol))


INPROJ_TN = 1024
STEPS_PER_SEG = D_MODEL // INPROJ_TN


def _inproj_kernel(x_ref, sh_ref, sc_ref, g_ref, w_ref, b_ref, cos_ref, sin_ref, o_ref, h_scr,
                   *, use_rope):
    j = pl.program_id(1)

    @pl.when(j == 0)
    def _():
        y = _rms(x_ref[...], g_ref[...])
        h_scr[...] = (y * (1.0 + sc_ref[...]) + sh_ref[...]).astype(BF16)

    seg = j // STEPS_PER_SEG

    def proj():
        return jnp.dot(h_scr[...], w_ref[...], preferred_element_type=F32) + b_ref[...]

    def rope(t):
        cos = cos_ref[...]
        sin = sin_ref[...]
        outs = []
        for hh in range(INPROJ_TN // HEAD_DIM):
            for part in range(HEAD_DIM // ROPE_HALF):
                lo = hh * HEAD_DIM + part * ROPE_HALF
                tt = t[:, lo:lo + ROPE_HALF]
                cs = cos[:, part * ROPE_HALF:(part + 1) * ROPE_HALF]
                sn = sin[:, part * ROPE_HALF:(part + 1) * ROPE_HALF]
                outs.append(tt * cs + pltpu.roll(tt, ROPE_FREQS, 1) * sn)
        return jnp.concatenate(outs, axis=1)

    @pl.when(seg == 0)
    def _():
        t = proj()
        o_ref[...] = (rope(t) if use_rope else t).astype(BF16)

    @pl.when(seg == 1)
    def _():
        t = proj()
        o_ref[...] = ((rope(t) if use_rope else t) * K_SCALE).astype(BF16)

    @pl.when((seg == 2) | (seg == 4))
    def _():
        o_ref[...] = proj().astype(BF16)

    @pl.when(seg == 3)
    def _():
        t = proj()
        o_ref[...] = (t * _sigmoid(t)).astype(BF16)

    @pl.when(seg == 5)
    def _():
        o_ref[...] = _gelu_tanh(proj()).astype(BF16)

    @pl.when(seg >= 6)
    def _():
        o_ref[...] = _sigmoid(proj()).astype(BF16)


def _in_proj(x2, mods, l, g_pre, w_in, b_in, cos_tab, sin_tab, *, tm, tiles_per_group, group0,
             use_rope):
    m = x2.shape[0]
    nt = m // tm
    tab_tiles = cos_tab.shape[0] // tm

    def mod_map(chunk):
        return lambda i, j: (l, group0 + i // tiles_per_group, 0, chunk)

    return pl.pallas_call(
        functools.partial(_inproj_kernel, use_rope=use_rope),
        grid=(nt, IN_COLS // INPROJ_TN),
        in_specs=[
            pl.BlockSpec((tm, D_MODEL), lambda i, j: (i, 0)),
            pl.BlockSpec((None, None, 1, D_MODEL), mod_map(0)),
            pl.BlockSpec((None, None, 1, D_MODEL), mod_map(1)),
            pl.BlockSpec((1, D_MODEL), lambda i, j: (0, 0)),
            pl.BlockSpec((None, D_MODEL, INPROJ_TN), lambda i, j: (l, 0, j)),
            pl.BlockSpec((1, INPROJ_TN), lambda i, j: (0, j)),
            pl.BlockSpec((tm, HEAD_DIM), lambda i, j: (i % tab_tiles, 0)),
            pl.BlockSpec((tm, HEAD_DIM), lambda i, j: (i % tab_tiles, 0)),
        ],
        out_specs=pl.BlockSpec((tm, INPROJ_TN), lambda i, j: (i, j)),
        out_shape=jax.ShapeDtypeStruct((m, IN_COLS), BF16),
        scratch_shapes=[pltpu.VMEM((tm, D_MODEL), BF16)],
        compiler_params=_cparams(("arbitrary", "arbitrary")),
        name="in_proj",
    )(x2, mods, mods, g_pre.reshape(1, D_MODEL), w_in, b_in.reshape(1, IN_COLS), cos_tab, sin_tab)


def _log_sigmoid(x):
    return jnp.minimum(x, 0.0) - jnp.log1p(jnp.exp(-jnp.abs(x)))


_TN_DIMS = (((0,), (0,)), ((), ()))
_NT_DIMS = (((1,), (1,)), ((), ()))


def _ret_bwd_kernel(dec_ref, k_ref, v_ref, s0_ref, sb_ref, sfin_ref, s_scr, dk_scr, ds_scr,
                    *, cpb):
    h = pl.program_id(1)
    j = pl.program_id(2)
    c = RET_CHUNK

    @pl.when(j == 0)
    def _():
        s_scr[...] = s0_ref[...]
        lg = _log_sigmoid(jnp.full((c, HEAD_DIM), dec_ref[1, h], F32))
        row = lax.broadcasted_iota(jnp.int32, (c, HEAD_DIM), 0).astype(F32)
        dk_scr[...] = jnp.exp(row * lg)
        ds_scr[...] = jnp.exp(float(c) * _log_sigmoid(jnp.full((SUB, HEAD_DIM), dec_ref[1, h], F32)))

    for cc in range(cpb - 1, -1, -1):
        sb_ref[cc] = s_scr[...].astype(BF16)
        k = k_ref[cc * c:(cc + 1) * c, :]
        v = v_ref[cc * c:(cc + 1) * c, :]
        vb = (v.astype(F32) * dk_scr[...]).astype(BF16)
        kv = lax.dot_general(k, vb, _TN_DIMS, preferred_element_type=F32)
        s_scr[...] = s_scr[...] * ds_scr[0:1, :] + kv

    @pl.when(j == pl.num_programs(2) - 1)
    def _():
        sfin_ref[...] = s_scr[...]


def _ret_fwd_kernel(dec_ref, q_ref, k_ref, v_ref, og_ref, sb_ref, s0_ref, gn_ref, y_ref, sfin_ref,
                    s_scr, mask_scr, dqf_scr, dqb_scr, dk_scr, ds_scr, *, cpb):
    h = pl.program_id(1)
    j = pl.program_id(2)
    c = RET_CHUNK

    @pl.when(j == 0)
    def _():
        s_scr[...] = s0_ref[...]
        lgf = _log_sigmoid(jnp.full((c, c), dec_ref[0, h], F32))
        lgb = _log_sigmoid(jnp.full((c, c), dec_ref[1, h], F32))
        ii = lax.broadcasted_iota(jnp.int32, (c, c), 0)
        jj = lax.broadcasted_iota(jnp.int32, (c, c), 1)
        d = (ii - jj).astype(F32)
        fwd = jnp.where(d >= 0.0, jnp.exp(jnp.maximum(d, 0.0) * lgf), 0.0)
        bwd = jnp.where(d <= 0.0, jnp.exp(jnp.maximum(-d, 0.0) * lgb), 0.0)
        mask_scr[...] = fwd + bwd
        row = ii.astype(F32)
        dqf_scr[...] = jnp.exp((row + 1.0) * lgf)
        dqb_scr[...] = jnp.exp((float(c) - row) * lgb)
        dk_scr[...] = jnp.exp((float(c) - 1.0 - row) * lgf)
        ds_scr[...] = jnp.exp(float(c) * _log_sigmoid(jnp.full((SUB, HEAD_DIM), dec_ref[0, h], F32)))

    for cc in range(cpb):
        sl = slice(cc * c, (cc + 1) * c)
        q = q_ref[sl, :]
        k = k_ref[sl, :]
        v = v_ref[sl, :]
        att = lax.dot_general(q, k, _NT_DIMS, preferred_element_type=F32)
        p = (att * mask_scr[...]).astype(BF16)
        o = jnp.dot(p, v, preferred_element_type=F32)
        o = o + dqf_scr[...] * jnp.dot(q, s_scr[...].astype(BF16), preferred_element_type=F32)
        o = o + dqb_scr[...] * jnp.dot(q, sb_ref[cc], preferred_element_type=F32)
        vf = (v.astype(F32) * dk_scr[...]).astype(BF16)
        kv = lax.dot_general(k, vf, _TN_DIMS, preferred_element_type=F32)
        s_scr[...] = s_scr[...] * ds_scr[0:1, :] + kv
        mu = jnp.mean(o, axis=-1, keepdims=True)
        oc = o - mu
        var = jnp.mean(oc * oc, axis=-1, keepdims=True)
        y = oc * lax.rsqrt(var + NORM_EPS) * gn_ref[...]
        y_ref[sl, :] = (y * og_ref[sl, :].astype(F32)).astype(BF16)

    @pl.when(j == pl.num_programs(2) - 1)
    def _():
        sfin_ref[...] = s_scr[...]


def _retention(proj, ret_decay, ret_gn, s0_f, s0_b, *, batch, n, cpb):
    c = RET_CHUNK
    rows = cpb * c
    nb = n // rows
    hd = HEAD_DIM
    state_spec = pl.BlockSpec((None, None, hd, hd), lambda b, h, j: (b, h, 0, 0))
    smem = pl.BlockSpec(memory_space=pltpu.SMEM)

    def col(seg, rev):
        if rev:
            return pl.BlockSpec((rows, hd), lambda b, h, j: (b * nb + nb - 1 - j, seg * RET_HEADS + h))
        return pl.BlockSpec((rows, hd), lambda b, h, j: (b * nb + j, seg * RET_HEADS + h))

    sb, sfin_b = pl.pallas_call(
        functools.partial(_ret_bwd_kernel, cpb=cpb),
        grid=(batch, RET_HEADS, nb),
        in_specs=[smem, col(1, True), col(2, True), state_spec],
        out_specs=[
            pl.BlockSpec((None, None, cpb, hd, hd), lambda b, h, j: (b, h, nb - 1 - j, 0, 0)),
            state_spec,
        ],
        out_shape=[
            jax.ShapeDtypeStruct((batch, RET_HEADS, nb * cpb, hd, hd), BF16),
            jax.ShapeDtypeStruct((batch, RET_HEADS, hd, hd), F32),
        ],
        scratch_shapes=[pltpu.VMEM((hd, hd), F32), pltpu.VMEM((c, hd), F32), pltpu.VMEM((SUB, hd), F32)],
        compiler_params=_cparams(("arbitrary", "arbitrary", "arbitrary")),
        name="ret_bwd",
    )(ret_decay, proj, proj, s0_b)

    y, sfin_f = pl.pallas_call(
        functools.partial(_ret_fwd_kernel, cpb=cpb),
        grid=(batch, RET_HEADS, nb),
        in_specs=[
            smem, col(0, False), col(1, False), col(2, False), col(3, False),
            pl.BlockSpec((None, None, cpb, hd, hd), lambda b, h, j: (b, h, j, 0, 0)),
            state_spec,
            pl.BlockSpec((1, hd), lambda b, h, j: (0, h)),
        ],
        out_specs=[
            pl.BlockSpec((rows, hd), lambda b, h, j: (b * nb + j, h)),
            state_spec,
        ],
        out_shape=[
            jax.ShapeDtypeStruct((batch * n, D_MODEL), BF16),
            jax.ShapeDtypeStruct((batch, RET_HEADS, hd, hd), F32),
        ],
        scratch_shapes=[
            pltpu.VMEM((hd, hd), F32), pltpu.VMEM((c, c), F32), pltpu.VMEM((c, hd), F32),
            pltpu.VMEM((c, hd), F32), pltpu.VMEM((c, hd), F32), pltpu.VMEM((SUB, hd), F32),
        ],
        compiler_params=_cparams(("arbitrary", "arbitrary", "arbitrary")),
        name="ret_fwd",
    )(ret_decay, proj, proj, proj, proj, sb, s0_f, ret_gn.reshape(1, D_MODEL))
    return y, sfin_f, sfin_b


HALO = 16
LOG2E = math.log2(math.e)
TINY = 1e-30


def _row_scan(a, u, carry, reverse):
    rows, w = a.shape
    nv = rows // SUB
    a3 = a.reshape(nv, SUB, w)
    u3 = u.reshape(nv, SUB, w)
    sub = lax.broadcasted_iota(jnp.int32, (nv, SUB, w), 1)
    for d in (1, 2, 4):
        valid = sub < SUB - d if reverse else sub >= d
        shift = SUB - d if reverse else d
        a_sh = jnp.where(valid, pltpu.roll(a3, shift, 1), 1.0)
        u_sh = jnp.where(valid, pltpu.roll(u3, shift, 1), 0.0)
        u3 = a3 * u_sh + u3
        a3 = a3 * a_sh
    edge = 0 if reverse else SUB - 1
    out = [None] * nv
    for v in (range(nv - 1, -1, -1) if reverse else range(nv)):
        e = u3[v] + a3[v] * carry
        out[v] = e
        carry = e[edge:edge + 1, :]
    return jnp.concatenate(out, axis=0)


def _lru_kernel(*refs, reverse, tt):
    if reverse:
        (xp_ref, x_ref, xn_ref, cw_ref, cb_ref, wa_ref, ba_ref, wx_ref, bx_ref, lam_ref, h0_ref,
         out_ref, hfin_ref, carry_scr, xs_scr) = refs
    else:
        (xp_ref, x_ref, xn_ref, cw_ref, cb_ref, wa_ref, ba_ref, wx_ref, bx_ref, lam_ref, h0_ref,
         hb_ref, gate_ref, out_ref, hfin_ref, carry_scr, xs_scr, os_scr) = refs
    i = pl.program_id(2)
    nt = pl.num_programs(2)
    ti = nt - 1 - i if reverse else i
    w = LRU_BLOCK_DIM
    g = tt // SUB
    slabs = w // LANES

    @pl.when(i == 0)
    def _():
        carry_scr[...] = h0_ref[...]

    x = x_ref[...].astype(F32)
    prev = jnp.where(ti == 0, 0.0, xp_ref[...].astype(F32)[HALO - SUB:HALO, :])
    nxt = jnp.where(ti == nt - 1, 0.0, xn_ref[...].astype(F32)[0:SUB, :])
    for s in range(slabs):
        ls = slice(s * LANES, (s + 1) * LANES)
        xs_scr[s, 0:SUB, :] = prev[:, ls]
        xs_scr[s, SUB:SUB + tt, :] = x[:, ls]
        xs_scr[s, SUB + tt:2 * SUB + tt, :] = nxt[:, ls]

    def phase(k):
        return jnp.concatenate(
            [xs_scr[s, pl.ds(SUB + k, g, stride=SUB), :] for s in range(slabs)], axis=1)

    ph = [phase(k) for k in range(-2, SUB + 1)]
    cw = cw_ref[...]
    cb = cb_ref[...]
    xc = jnp.concatenate(
        [cw[0:1, :] * ph[k] + cw[1:2, :] * ph[k + 1] + cw[2:3, :] * ph[k + 2]
         + cw[3:4, :] * ph[k + 3] + cb for k in range(SUB)], axis=0)

    xb = xc.astype(BF16)
    tr = jnp.tanh(jnp.dot(xb, wa_ref[...], preferred_element_type=F32) + 0.5 * ba_ref[...])
    tg = jnp.tanh(jnp.dot(xb, wx_ref[...], preferred_element_type=F32) + 0.5 * bx_ref[...])
    k2h = (-0.5 * LRU_C * LOG2E) * _softplus(-lam_ref[...])
    a = jnp.exp2(tr * k2h + k2h)
    y = 1.0 - a * a
    root = y * lax.rsqrt(jnp.maximum(y, TINY))
    u = root * ((0.5 * tg + 0.5) * xc)

    order = range(SUB - 1, -1, -1) if reverse else range(SUB)
    hloc = [None] * SUB
    ploc = [None] * SUB
    hp = pp = None
    for k in order:
        ak = a[k * g:(k + 1) * g, :]
        uk = u[k * g:(k + 1) * g, :]
        hp = uk if hp is None else ak * hp + uk
        pp = ak if pp is None else ak * pp
        hloc[k] = hp
        ploc[k] = pp

    carry = carry_scr[...]
    e = _row_scan(pp, hp, carry, reverse)
    rowi = lax.broadcasted_iota(jnp.int32, (g, w), 0)
    if reverse:
        cin = jnp.where(rowi == g - 1, carry, pltpu.roll(e, g - 1, 0))
        new_carry = e[0:1, :]
    else:
        cin = jnp.where(rowi == 0, carry, pltpu.roll(e, 1, 0))
        new_carry = e[g - 1:g, :]
    carry_scr[...] = new_carry

    if reverse:
        for k in range(SUB):
            out_ref[k * g:(k + 1) * g, :] = (hloc[k] + ploc[k] * cin).astype(BF16)
    else:
        for k in range(SUB):
            hk = hloc[k] + ploc[k] * cin + hb_ref[k * g:(k + 1) * g, :].astype(F32)
            for s in range(slabs):
                os_scr[s, pl.ds(k, g, stride=SUB), :] = hk[:, s * LANES:(s + 1) * LANES]
        tok = jnp.concatenate([os_scr[s] for s in range(slabs)], axis=1)
        out_ref[...] = (tok * gate_ref[...].astype(F32)).astype(BF16)

    @pl.when(i == nt - 1)
    def _():
        hfin_ref[...] = new_carry


def _rglru(proj, l, conv_w, conv_b, lru_wa, lru_ba, lru_wx, lru_bx, lru_lambda, h0_f, h0_b,
           *, batch, n, tt):
    w = LRU_BLOCK_DIM
    nt = n // tt
    hpt = tt // HALO
    nh = n // HALO
    xseg = 4 * LRU_BLOCKS
    gseg = 5 * LRU_BLOCKS
    slabs = w // LANES

    def specs(rev):
        d = 1 if rev else 0

        def t_of(i):
            return nt - 1 - i if rev else i

        def vec(arr_rows):
            return pl.BlockSpec((arr_rows, w), lambda b, cb, i: (0, cb))

        wspec = pl.BlockSpec((None, None, None, w, w), lambda b, cb, i: (l, d, cb, 0, 0))
        return [
            pl.BlockSpec((HALO, w), lambda b, cb, i: (b * nh + jnp.maximum(t_of(i) * hpt - 1, 0), xseg + cb)),
            pl.BlockSpec((tt, w), lambda b, cb, i: (b * nt + t_of(i), xseg + cb)),
            pl.BlockSpec((HALO, w), lambda b, cb, i: (b * nh + jnp.minimum((t_of(i) + 1) * hpt, nh - 1), xseg + cb)),
            vec(4), vec(1), wspec, vec(1), wspec, vec(1), vec(1),
            pl.BlockSpec((None, 1, w), lambda b, cb, i: (b, 0, cb)),
        ]

    stage = pltpu.VMEM((slabs, tt + 2 * SUB, LANES), F32)
    state_spec = pl.BlockSpec((None, 1, w), lambda b, cb, i: (b, 0, cb))
    state_shape = jax.ShapeDtypeStruct((batch, 1, D_MODEL), F32)
    grid = (batch, LRU_BLOCKS, nt)
    params = _cparams(("arbitrary", "arbitrary", "arbitrary"))

    def small(d):
        return (conv_w, conv_b.reshape(1, -1), lru_wa, lru_ba[d].reshape(1, -1), lru_wx,
                lru_bx[d].reshape(1, -1), lru_lambda[d].reshape(1, -1))

    hb, hfin_b = pl.pallas_call(
        functools.partial(_lru_kernel, reverse=True, tt=tt),
        grid=grid,
        in_specs=specs(True),
        out_specs=[pl.BlockSpec((tt, w), lambda b, cb, i: (b * nt + nt - 1 - i, cb)), state_spec],
        out_shape=[jax.ShapeDtypeStruct((batch * n, D_MODEL), BF16), state_shape],
        scratch_shapes=[pltpu.VMEM((1, w), F32), stage],
        compiler_params=params,
        name="lru_bwd",
    )(proj, proj, proj, *small(1), h0_b)

    y, hfin_f = pl.pallas_call(
        functools.partial(_lru_kernel, reverse=False, tt=tt),
        grid=grid,
        in_specs=specs(False) + [
            pl.BlockSpec((tt, w), lambda b, cb, i: (b * nt + i, cb)),
            pl.BlockSpec((tt, w), lambda b, cb, i: (b * nt + i, gseg + cb)),
        ],
        out_specs=[pl.BlockSpec((tt, w), lambda b, cb, i: (b * nt + i, cb)), state_spec],
        out_shape=[jax.ShapeDtypeStruct((batch * n, D_MODEL), BF16), state_shape],
        scratch_shapes=[pltpu.VMEM((1, w), F32), stage, pltpu.VMEM((slabs, tt, LANES), F32)],
        compiler_params=params,
        name="lru_fwd",
    )(proj, proj, proj, *small(0), h0_f, hb, proj)
    return y, hfin_f, hfin_b


MERGE_TM = 256


def _merge_kernel(yr_ref, yl_ref, sa_ref, sb_ref, x_ref, ga_ref, g_ref, wr_ref, wl_ref, wo_ref,
                  o_ref):
    pr = jnp.dot(yr_ref[...], wr_ref[...], preferred_element_type=F32)
    plru = jnp.dot(yl_ref[...], wl_ref[...], preferred_element_type=F32)
    y = sa_ref[...].astype(F32) * pr + sb_ref[...].astype(F32) * plru
    z = jnp.dot(y.astype(BF16), wo_ref[...], preferred_element_type=F32)
    o_ref[...] = x_ref[...] + ga_ref[...] * _rms(z, g_ref[...])


def _merge(y_r, y_l, proj, x2, mods, l, g_post, w_ret_o, w_lru_o, w_out, *, tiles_per_group, group0):
    m = x2.shape[0]
    tm = MERGE_TM
    row = lambda i: (i, 0)
    wspec = pl.BlockSpec((None, D_MODEL, D_MODEL), lambda i: (l, 0, 0), pipeline_mode=pl.Buffered(1))
    return pl.pallas_call(
        _merge_kernel,
        grid=(m // tm,),
        in_specs=[
            pl.BlockSpec((tm, D_MODEL), row),
            pl.BlockSpec((tm, D_MODEL), row),
            pl.BlockSpec((tm, D_MODEL), lambda i: (i, 6)),
            pl.BlockSpec((tm, D_MODEL), lambda i: (i, 7)),
            pl.BlockSpec((tm, D_MODEL), row),
            pl.BlockSpec((None, None, 1, D_MODEL), lambda i: (l, group0 + i // tiles_per_group, 0, 2)),
            pl.BlockSpec((1, D_MODEL), lambda i: (0, 0)),
            wspec, wspec, wspec,
        ],
        out_specs=pl.BlockSpec((tm, D_MODEL), row),
        out_shape=jax.ShapeDtypeStruct((m, D_MODEL), F32),
        compiler_params=_cparams(("arbitrary",)),
        name="merge",
    )(y_r, y_l, proj, proj, x2, mods, g_post.reshape(1, D_MODEL), w_ret_o, w_lru_o, w_out)


FFN_TH = 512
FFN_STEPS = FFN_HIDDEN // FFN_TH
FFN_SUB_ROWS = 512


def _ffn_kernel(x_ref, sh_ref, sc_ref, ga_ref, gpre_ref, gpost_ref, wa_ref, wg_ref, wo_ref, o_ref,
                h_scr):
    j = pl.program_id(1)

    @pl.when(j == 0)
    def _():
        y = _rms(x_ref[...], gpre_ref[...])
        h_scr[...] = (y * (1.0 + sc_ref[...]) + sh_ref[...]).astype(BF16)
        o_ref[...] = jnp.zeros_like(o_ref)

    sub_rows = min(FFN_SUB_ROWS, o_ref.shape[0])
    for r in range(0, o_ref.shape[0], sub_rows):
        rs = slice(r, r + sub_rows)
        h = h_scr[rs, :]
        a = jnp.dot(h, wa_ref[...], preferred_element_type=F32)
        g = jnp.dot(h, wg_ref[...], preferred_element_type=F32)
        act = (a * _sigmoid(a) * g).astype(BF16)
        o_ref[rs, :] += jnp.dot(act, wo_ref[...], preferred_element_type=F32)

    @pl.when(j == pl.num_programs(1) - 1)
    def _():
        o_ref[...] = x_ref[...] + ga_ref[...] * _rms(o_ref[...], gpost_ref[...])


def _ffn(x2, mods, l, g_pre, g_post, w_ffn_in, w_ffn_out, *, tm, tiles_per_group, group0):
    m = x2.shape[0]

    def mod_map(chunk):
        return lambda i, j: (l, group0 + i // tiles_per_group, 0, chunk)

    return pl.pallas_call(
        _ffn_kernel,
        grid=(m // tm, FFN_STEPS),
        in_specs=[
            pl.BlockSpec((tm, D_MODEL), lambda i, j: (i, 0), pipeline_mode=pl.Buffered(1)),
            pl.BlockSpec((None, None, 1, D_MODEL), mod_map(3)),
            pl.BlockSpec((None, None, 1, D_MODEL), mod_map(4)),
            pl.BlockSpec((None, None, 1, D_MODEL), mod_map(5)),
            pl.BlockSpec((1, D_MODEL), lambda i, j: (0, 0)),
            pl.BlockSpec((1, D_MODEL), lambda i, j: (0, 0)),
            pl.BlockSpec((None, D_MODEL, FFN_TH), lambda i, j: (l, 0, j)),
            pl.BlockSpec((None, D_MODEL, FFN_TH), lambda i, j: (l, 0, FFN_STEPS + j)),
            pl.BlockSpec((None, FFN_TH, D_MODEL), lambda i, j: (l, j, 0)),
        ],
        out_specs=pl.BlockSpec((tm, D_MODEL), lambda i, j: (i, 0)),
        out_shape=jax.ShapeDtypeStruct((m, D_MODEL), F32),
        scratch_shapes=[pltpu.VMEM((tm, D_MODEL), BF16)],
        compiler_params=_cparams(("arbitrary", "arbitrary")),
        name="ffn",
    )(x2, mods, mods, mods, g_pre.reshape(1, D_MODEL), g_post.reshape(1, D_MODEL),
      w_ffn_in, w_ffn_in, w_ffn_out)


def _rope_tables(n):
    t = jnp.arange(n, dtype=jnp.int32)
    row = (t // GRID_W).astype(F32)
    col = (t % GRID_W).astype(F32)
    inv = ROPE_BASE ** (-jnp.arange(ROPE_FREQS, dtype=F32) / ROPE_FREQS)
    ar = row[:, None] * inv[None, :]
    ac = col[:, None] * inv[None, :]
    cos = jnp.concatenate([jnp.cos(ar), jnp.cos(ar), jnp.cos(ac), jnp.cos(ac)], axis=1)
    sin = jnp.concatenate([-jnp.sin(ar), jnp.sin(ar), -jnp.sin(ac), jnp.sin(ac)], axis=1)
    return cos, sin


def _pick_tile(n, pref):
    t = pref
    while n % t:
        t //= 2
    return t


def kernel(x, c, ctx, c_ctx, w_mod, b_mod, g_pre_mix, g_post_mix, g_pre_ffn, g_post_ffn, w_in, b_in,
           ret_decay, ret_gn, conv_w, conv_b, lru_wa, lru_ba, lru_wx, lru_bx, lru_lambda, w_ret_o,
           w_lru_o, w_out, w_ffn_in, w_ffn_out):
    batch, n, d = x.shape
    n_ctx = ctx.shape[1]
    depth = w_mod.shape[0]
    assert d == D_MODEL and n % RET_CHUNK == 0 and n_ctx % RET_CHUNK == 0
    assert RET_CHUNK == HEAD_DIM

    mod_rows = SUB * ((batch + 1 + SUB - 1) // SUB)
    c_rows = jnp.zeros((mod_rows, D_MODEL), F32).at[:batch].set(c).at[batch].set(c_ctx)
    mods = _modulation(c_rows, w_mod, b_mod).reshape(depth, mod_rows, 1, 6 * D_MODEL)

    cos_tab, sin_tab = _rope_tables(n)
    cos_ctx = jnp.ones((batch * n_ctx, HEAD_DIM), F32)
    sin_ctx = jnp.zeros((batch * n_ctx, HEAD_DIM), F32)

    tm_lat = _pick_tile(n, 1024)
    tm_ffn = _pick_tile(n, 1024)
    tm_ctx = batch * n_ctx
    tt_lat = _pick_tile(n, 2048)
    cpb_lat = _pick_tile(n // RET_CHUNK, 8)

    x2 = x.reshape(batch * n, D_MODEL)
    ctx2 = ctx.reshape(batch * n_ctx, D_MODEL)

    w_in = w_in.astype(BF16)
    lru_wa, lru_wx = (0.5 * lru_wa).astype(BF16), (0.5 * lru_wx).astype(BF16)
    w_ret_o, w_lru_o, w_out = w_ret_o.astype(BF16), w_lru_o.astype(BF16), w_out.astype(BF16)
    w_ffn_in, w_ffn_out = w_ffn_in.astype(BF16), w_ffn_out.astype(BF16)

    zeros_s = jnp.zeros((batch, RET_HEADS, HEAD_DIM, HEAD_DIM), F32)
    zeros_h = jnp.zeros((batch, 1, D_MODEL), F32)

    for l in range(depth):
        with_ctx = l < depth - 1
        lru_args = (conv_w[l], conv_b[l], lru_wa, lru_ba[l], lru_wx, lru_bx[l], lru_lambda[l])

        proj_c = _in_proj(ctx2, mods, l, g_pre_mix[l], w_in, b_in[l], cos_ctx, sin_ctx, tm=tm_ctx,
                          tiles_per_group=1, group0=batch, use_rope=False)
        yr_c, s_f, s_b = _retention(proj_c, ret_decay[l], ret_gn[l], zeros_s, zeros_s,
                                    batch=batch, n=n_ctx, cpb=n_ctx // RET_CHUNK)
        yl_c, h_f, h_b = _rglru(proj_c, l, *lru_args, zeros_h, zeros_h, batch=batch, n=n_ctx, tt=n_ctx)

        proj = _in_proj(x2, mods, l, g_pre_mix[l], w_in, b_in[l], cos_tab, sin_tab, tm=tm_lat,
                        tiles_per_group=n // tm_lat, group0=0, use_rope=True)
        y_r, _, _ = _retention(proj, ret_decay[l], ret_gn[l], s_f, s_b, batch=batch, n=n, cpb=cpb_lat)
        y_l, _, _ = _rglru(proj, l, *lru_args, h_f, h_b, batch=batch, n=n, tt=tt_lat)

        x2 = _merge(y_r, y_l, proj, x2, mods, l, g_post_mix[l], w_ret_o, w_lru_o, w_out,
                    tiles_per_group=n // MERGE_TM, group0=0)
        x2 = _ffn(x2, mods, l, g_pre_ffn[l], g_post_ffn[l], w_ffn_in, w_ffn_out, tm=tm_ffn,
                  tiles_per_group=n // tm_ffn, group0=0)
        if with_ctx:
            ctx2 = _merge(yr_c, yl_c, proj_c, ctx2, mods, l, g_post_mix[l], w_ret_o, w_lru_o, w_out,
                          tiles_per_group=batch * n_ctx // MERGE_TM, group0=batch)
            ctx2 = _ffn(ctx2, mods, l, g_pre_ffn[l], g_post_ffn[l], w_ffn_in, w_ffn_out, tm=tm_ctx,
                        tiles_per_group=1, group0=batch)
    return x2.reshape(batch, n, D_MODEL)
```

```python
import functools
import math

import jax
import jax.numpy as jnp
from jax import lax
from jax.experimental import pallas as pl
from jax.experimental.pallas import tpu as pltpu

F32 = jnp.float32
BF16 = jnp.bfloat16

D_MODEL = 2048
GRID_W = 64
RET_HEADS = 8
HEAD_DIM = D_MODEL // RET_HEADS
ROPE_HALF = HEAD_DIM // 2
ROPE_FREQS = ROPE_HALF // 2
ROPE_BASE = 10000.0
LRU_BLOCKS = 8
LRU_BLOCK_DIM = D_MODEL // LRU_BLOCKS
LRU_C = 8.0
FFN_HIDDEN = ((8 * D_MODEL // 3 + 255) // 256) * 256
NORM_EPS = 1e-6
N_SEG = 8
IN_COLS = N_SEG * D_MODEL
K_SCALE = HEAD_DIM ** -0.5

LANES = 128
SUB = 8
RET_CHUNK = 256
VMEM_LIMIT = 58 * 1024 * 1024


def _cparams(sem):
    return pltpu.CompilerParams(dimension_semantics=sem, vmem_limit_bytes=VMEM_LIMIT)


def _rms(x, g):
    ms = jnp.mean(x * x, axis=-1, keepdims=True)
    return x * lax.rsqrt(ms + NORM_EPS) * g


def _gelu_tanh(x):
    c = 0.7978845608028654
    return 0.5 * x * (1.0 + jnp.tanh(c * (x + 0.044715 * (x * x * x))))


def _sigmoid(x):
    return 0.5 * jnp.tanh(0.5 * x) + 0.5


def _softplus(x):
    return jnp.maximum(x, 0.0) + jnp.log1p(jnp.exp(-jnp.abs(x)))


MOD_TN = 1024


def _mod_kernel(c_ref, w_ref, b_ref, o_ref):
    c = c_ref[...]
    s = (c * jax.nn.sigmoid(c)).astype(BF16)
    o_ref[...] = jnp.dot(s, w_ref[...].astype(BF16), preferred_element_type=F32) + b_ref[...]


def _modulation(c_rows, w_mod, b_mod):
    depth = w_mod.shape[0]
    rows = c_rows.shape[0]
    ncol = w_mod.shape[2]
    return pl.pallas_call(
        _mod_kernel,
        grid=(depth, ncol // MOD_TN),
        in_specs=[
            pl.BlockSpec((rows, D_MODEL), lambda l, j: (0, 0)),
            pl.BlockSpec((None, D_MODEL, MOD_TN), lambda l, j: (l, 0, j)),
            pl.BlockSpec((None, 1, MOD_TN), lambda l, j: (l, 0, j)),
        ],
        out_specs=pl.BlockSpec((None, rows, MOD_TN), lambda l, j: (l, 0, j)),
        out_shape=jax.ShapeDtypeStruct((depth, rows, ncol), F32),
        compiler_params=_cparams(("arbitrary", "arbitrary")),
        name="modulation",
    )(c_rows, w_mod, b_mod.reshape(depth, 1, ncol))


INPROJ_TN = 1024
STEPS_PER_SEG = D_MODEL // INPROJ_TN
INPROJ_EDGE_ROWS = 256
INPROJ_ROPE_ROWS = 512


def _inproj_kernel(x_ref, sh_ref, sc_ref, g_ref, w_ref, b_ref, cos_ref, sin_ref, o_ref, h_scr,
                   *, use_rope):
    j = pl.program_id(1)
    rows = o_ref.shape[0]
    seg = j // STEPS_PER_SEG

    def rope(t, rs):
        if not use_rope:
            return t
        cos = cos_ref[rs, :]
        sin = sin_ref[rs, :]
        outs = []
        for hh in range(INPROJ_TN // HEAD_DIM):
            for part in range(HEAD_DIM // ROPE_HALF):
                lo = hh * HEAD_DIM + part * ROPE_HALF
                tt = t[:, lo:lo + ROPE_HALF]
                cs = cos[:, part * ROPE_HALF:(part + 1) * ROPE_HALF]
                sn = sin[:, part * ROPE_HALF:(part + 1) * ROPE_HALF]
                outs.append(tt * cs + pltpu.roll(tt, ROPE_FREQS, 1) * sn)
        return jnp.concatenate(outs, axis=1)

    def emit(epilogue, first=False, sub_rows=rows):
        for r in range(0, rows, sub_rows):
            rs = slice(r, r + sub_rows)
            if first:
                y = _rms(x_ref[rs, :], g_ref[...])
                h = (y * (1.0 + sc_ref[...]) + sh_ref[...]).astype(BF16)
                h_scr[rs, :] = h
            else:
                h = h_scr[rs, :]
            t = jnp.dot(h, w_ref[...], preferred_element_type=F32) + b_ref[...]
            o_ref[rs, :] = epilogue(t, rs).astype(BF16)

    edge_rows = min(INPROJ_EDGE_ROWS, rows)
    pl.when(j == 0)(lambda: emit(rope, first=True, sub_rows=edge_rows))
    rope_rows = min(INPROJ_ROPE_ROWS, rows)
    pl.when((seg == 0) & (j > 0))(lambda: emit(rope, sub_rows=rope_rows))
    pl.when(seg == 1)(lambda: emit(lambda t, rs: rope(t, rs) * K_SCALE, sub_rows=rope_rows))
    pl.when((seg == 2) | (seg == 4))(lambda: emit(lambda t, rs: t))
    pl.when(seg == 3)(lambda: emit(lambda t, rs: t * _sigmoid(t)))
    pl.when(seg == 5)(lambda: emit(lambda t, rs: _gelu_tanh(t)))
    pl.when(seg >= 6)(lambda: emit(lambda t, rs: _sigmoid(t)))


def _in_proj(x2, mods, l, g_pre, w_in, b_in, cos_tab, sin_tab, *, tm, tiles_per_group, group0,
             use_rope):
    m = x2.shape[0]
    nt = m // tm
    tab_tiles = cos_tab.shape[0] // tm

    def mod_map(chunk):
        return lambda i, j: (l, group0 + i // tiles_per_group, 0, chunk)

    return pl.pallas_call(
        functools.partial(_inproj_kernel, use_rope=use_rope),
        grid=(nt, IN_COLS // INPROJ_TN),
        in_specs=[
            pl.BlockSpec((tm, D_MODEL), lambda i, j: (i, 0)),
            pl.BlockSpec((None, None, 1, D_MODEL), mod_map(0)),
            pl.BlockSpec((None, None, 1, D_MODEL), mod_map(1)),
            pl.BlockSpec((1, D_MODEL), lambda i, j: (0, 0)),
            pl.BlockSpec((None, D_MODEL, INPROJ_TN), lambda i, j: (l, 0, j)),
            pl.BlockSpec((1, INPROJ_TN), lambda i, j: (0, j)),
            pl.BlockSpec((tm, HEAD_DIM), lambda i, j: (i % tab_tiles, 0)),
            pl.BlockSpec((tm, HEAD_DIM), lambda i, j: (i % tab_tiles, 0)),
        ],
        out_specs=pl.BlockSpec((tm, INPROJ_TN), lambda i, j: (i, j)),
        out_shape=jax.ShapeDtypeStruct((m, IN_COLS), BF16),
        scratch_shapes=[pltpu.VMEM((tm, D_MODEL), BF16)],
        compiler_params=_cparams(("arbitrary", "arbitrary")),
        name="in_proj",
    )(x2, mods, mods, g_pre.reshape(1, D_MODEL), w_in, b_in.reshape(1, IN_COLS), cos_tab, sin_tab)


def _log_sigmoid(x):
    return jnp.minimum(x, 0.0) - jnp.log1p(jnp.exp(-jnp.abs(x)))


_TN_DIMS = (((0,), (0,)), ((), ()))
_NT_DIMS = (((1,), (1,)), ((), ()))


def _ret_bwd_kernel(dec_ref, k_ref, v_ref, s0_ref, sb_ref, sfin_ref, s_scr, dk_scr, ds_scr,
                    *, cpb):
    h = pl.program_id(1)
    j = pl.program_id(2)
    c = RET_CHUNK

    @pl.when(j == 0)
    def _():
        s_scr[...] = s0_ref[...]
        lg = _log_sigmoid(jnp.full((c, HEAD_DIM), dec_ref[1, h], F32))
        row = lax.broadcasted_iota(jnp.int32, (c, HEAD_DIM), 0).astype(F32)
        dk_scr[...] = jnp.exp(row * lg)
        ds_scr[...] = jnp.exp(float(c) * _log_sigmoid(jnp.full((SUB, HEAD_DIM), dec_ref[1, h], F32)))

    for cc in range(cpb - 1, -1, -1):
        sb_ref[cc] = s_scr[...].astype(BF16)
        k = k_ref[cc * c:(cc + 1) * c, :]
        v = v_ref[cc * c:(cc + 1) * c, :]
        vb = (v.astype(F32) * dk_scr[...]).astype(BF16)
        kv = lax.dot_general(k, vb, _TN_DIMS, preferred_element_type=F32)
        s_scr[...] = s_scr[...] * ds_scr[0:1, :] + kv

    @pl.when(j == pl.num_programs(2) - 1)
    def _():
        sfin_ref[...] = s_scr[...]


def _ret_fwd_kernel(dec_ref, q_ref, k_ref, v_ref, og_ref, sb_ref, s0_ref, gn_ref, y_ref, sfin_ref,
                    s_scr, mask_scr, dqf_scr, dqb_scr, dk_scr, ds_scr, *, cpb):
    h = pl.program_id(1)
    j = pl.program_id(2)
    c = RET_CHUNK

    @pl.when(j == 0)
    def _():
        s_scr[...] = s0_ref[...]
        lgf = _log_sigmoid(jnp.full((c, c), dec_ref[0, h], F32))
        lgb = _log_sigmoid(jnp.full((c, c), dec_ref[1, h], F32))
        ii = lax.broadcasted_iota(jnp.int32, (c, c), 0)
        jj = lax.broadcasted_iota(jnp.int32, (c, c), 1)
        d = (ii - jj).astype(F32)
        fwd = jnp.where(d >= 0.0, jnp.exp(jnp.maximum(d, 0.0) * lgf), 0.0)
        bwd = jnp.where(d <= 0.0, jnp.exp(jnp.maximum(-d, 0.0) * lgb), 0.0)
        mask_scr[...] = fwd + bwd
        row = ii.astype(F32)
        dqf_scr[...] = jnp.exp((row + 1.0) * lgf)
        dqb_scr[...] = jnp.exp((float(c) - row) * lgb)
        dk_scr[...] = jnp.exp((float(c) - 1.0 - row) * lgf)
        ds_scr[...] = jnp.exp(float(c) * _log_sigmoid(jnp.full((SUB, HEAD_DIM), dec_ref[0, h], F32)))

    for cc in range(cpb):
        sl = slice(cc * c, (cc + 1) * c)
        q = q_ref[sl, :]
        k = k_ref[sl, :]
        v = v_ref[sl, :]
        att = lax.dot_general(q, k, _NT_DIMS, preferred_element_type=F32)
        p = (att * mask_scr[...]).astype(BF16)
        o = jnp.dot(p, v, preferred_element_type=F32)
        o = o + dqf_scr[...] * jnp.dot(q, s_scr[...].astype(BF16), preferred_element_type=F32)
        o = o + dqb_scr[...] * jnp.dot(q, sb_ref[cc], preferred_element_type=F32)
        vf = (v.astype(F32) * dk_scr[...]).astype(BF16)
        kv = lax.dot_general(k, vf, _TN_DIMS, preferred_element_type=F32)
        s_scr[...] = s_scr[...] * ds_scr[0:1, :] + kv
        mu = jnp.mean(o, axis=-1, keepdims=True)
        oc = o - mu
        var = jnp.mean(oc * oc, axis=-1, keepdims=True)
        y = oc * lax.rsqrt(var + NORM_EPS) * gn_ref[...]
        y_ref[sl, :] = (y * og_ref[sl, :].astype(F32)).astype(BF16)

    @pl.when(j == pl.num_programs(2) - 1)
    def _():
        sfin_ref[...] = s_scr[...]


def _retention(proj, ret_decay, ret_gn, s0_f, s0_b, *, batch, n, cpb):
    c = RET_CHUNK
    rows = cpb * c
    nb = n // rows
    hd = HEAD_DIM
    state_spec = pl.BlockSpec((None, None, hd, hd), lambda b, h, j: (b, h, 0, 0))
    smem = pl.BlockSpec(memory_space=pltpu.SMEM)

    def col(seg, rev):
        if rev:
            return pl.BlockSpec((rows, hd), lambda b, h, j: (b * nb + nb - 1 - j, seg * RET_HEADS + h))
        return pl.BlockSpec((rows, hd), lambda b, h, j: (b * nb + j, seg * RET_HEADS + h))

    sb, sfin_b = pl.pallas_call(
        functools.partial(_ret_bwd_kernel, cpb=cpb),
        grid=(batch, RET_HEADS, nb),
        in_specs=[smem, col(1, True), col(2, True), state_spec],
        out_specs=[
            pl.BlockSpec((None, None, cpb, hd, hd), lambda b, h, j: (b, h, nb - 1 - j, 0, 0)),
            state_spec,
        ],
        out_shape=[
            jax.ShapeDtypeStruct((batch, RET_HEADS, nb * cpb, hd, hd), BF16),
            jax.ShapeDtypeStruct((batch, RET_HEADS, hd, hd), F32),
        ],
        scratch_shapes=[pltpu.VMEM((hd, hd), F32), pltpu.VMEM((c, hd), F32), pltpu.VMEM((SUB, hd), F32)],
        compiler_params=_cparams(("arbitrary", "arbitrary", "arbitrary")),
        name="ret_bwd",
    )(ret_decay, proj, proj, s0_b)

    y, sfin_f = pl.pallas_call(
        functools.partial(_ret_fwd_kernel, cpb=cpb),
        grid=(batch, RET_HEADS, nb),
        in_specs=[
            smem, col(0, False), col(1, False), col(2, False), col(3, False),
            pl.BlockSpec((None, None, cpb, hd, hd), lambda b, h, j: (b, h, j, 0, 0)),
            state_spec,
            pl.BlockSpec((1, hd), lambda b, h, j: (0, h)),
        ],
        out_specs=[
            pl.BlockSpec((rows, hd), lambda b, h, j: (b * nb + j, h)),
            state_spec,
        ],
        out_shape=[
            jax.ShapeDtypeStruct((batch * n, D_MODEL), BF16),
            jax.ShapeDtypeStruct((batch, RET_HEADS, hd, hd), F32),
        ],
        scratch_shapes=[
            pltpu.VMEM((hd, hd), F32), pltpu.VMEM((c, c), F32), pltpu.VMEM((c, hd), F32),
            pltpu.VMEM((c, hd), F32), pltpu.VMEM((c, hd), F32), pltpu.VMEM((SUB, hd), F32),
        ],
        compiler_params=_cparams(("arbitrary", "arbitrary", "arbitrary")),
        name="ret_fwd",
    )(ret_decay, proj, proj, proj, proj, sb, s0_f, ret_gn.reshape(1, D_MODEL))
    return y, sfin_f, sfin_b


HALO = 16
LOG2E = math.log2(math.e)
TINY = 1e-30


def _row_scan(a, u, carry, reverse):
    rows, w = a.shape
    nv = rows // SUB
    a3 = a.reshape(nv, SUB, w)
    u3 = u.reshape(nv, SUB, w)
    sub = lax.broadcasted_iota(jnp.int32, (nv, SUB, w), 1)
    for d in (1, 2, 4):
        valid = sub < SUB - d if reverse else sub >= d
        shift = SUB - d if reverse else d
        a_sh = jnp.where(valid, pltpu.roll(a3, shift, 1), 1.0)
        u_sh = jnp.where(valid, pltpu.roll(u3, shift, 1), 0.0)
        u3 = a3 * u_sh + u3
        a3 = a3 * a_sh
    edge = 0 if reverse else SUB - 1
    out = [None] * nv
    for v in (range(nv - 1, -1, -1) if reverse else range(nv)):
        e = u3[v] + a3[v] * carry
        out[v] = e
        carry = e[edge:edge + 1, :]
    return jnp.concatenate(out, axis=0)


def _lru_kernel(*refs, reverse, tt):
    if reverse:
        (xp_ref, x_ref, xn_ref, cw_ref, cb_ref, wa_ref, ba_ref, wx_ref, bx_ref, lam_ref, h0_ref,
         out_ref, hfin_ref, carry_scr, xs_scr) = refs
    else:
        (xp_ref, x_ref, xn_ref, cw_ref, cb_ref, wa_ref, ba_ref, wx_ref, bx_ref, lam_ref, h0_ref,
         hb_ref, gate_ref, out_ref, hfin_ref, carry_scr, xs_scr, os_scr) = refs
    i = pl.program_id(2)
    nt = pl.num_programs(2)
    ti = nt - 1 - i if reverse else i
    w = LRU_BLOCK_DIM
    g = tt // SUB
    slabs = w // LANES

    @pl.when(i == 0)
    def _():
        carry_scr[...] = h0_ref[...]

    x = x_ref[...].astype(F32)
    prev = jnp.where(ti == 0, 0.0, xp_ref[...].astype(F32)[HALO - SUB:HALO, :])
    nxt = jnp.where(ti == nt - 1, 0.0, xn_ref[...].astype(F32)[0:SUB, :])
    for s in range(slabs):
        ls = slice(s * LANES, (s + 1) * LANES)
        xs_scr[s, 0:SUB, :] = prev[:, ls]
        xs_scr[s, SUB:SUB + tt, :] = x[:, ls]
        xs_scr[s, SUB + tt:2 * SUB + tt, :] = nxt[:, ls]

    def phase(k):
        return jnp.concatenate(
            [xs_scr[s, pl.ds(SUB + k, g, stride=SUB), :] for s in range(slabs)], axis=1)

    ph = [phase(k) for k in range(-2, SUB + 1)]
    cw = cw_ref[...]
    cb = cb_ref[...]
    xc = jnp.concatenate(
        [cw[0:1, :] * ph[k] + cw[1:2, :] * ph[k + 1] + cw[2:3, :] * ph[k + 2]
         + cw[3:4, :] * ph[k + 3] + cb for k in range(SUB)], axis=0)

    xb = xc.astype(BF16)
    tr = jnp.tanh(jnp.dot(xb, wa_ref[...], preferred_element_type=F32) + 0.5 * ba_ref[...])
    tg = jnp.tanh(jnp.dot(xb, wx_ref[...], preferred_element_type=F32) + 0.5 * bx_ref[...])
    k2h = (-0.5 * LRU_C * LOG2E) * _softplus(-lam_ref[...])
    a = jnp.exp2(tr * k2h + k2h)
    y = 1.0 - a * a
    root = y * lax.rsqrt(jnp.maximum(y, TINY))
    u = root * ((0.5 * tg + 0.5) * xc)

    order = range(SUB - 1, -1, -1) if reverse else range(SUB)
    hloc = [None] * SUB
    ploc = [None] * SUB
    hp = pp = None
    for k in order:
        ak = a[k * g:(k + 1) * g, :]
        uk = u[k * g:(k + 1) * g, :]
        hp = uk if hp is None else ak * hp + uk
        pp = ak if pp is None else ak * pp
        hloc[k] = hp
        ploc[k] = pp

    carry = carry_scr[...]
    e = _row_scan(pp, hp, carry, reverse)
    rowi = lax.broadcasted_iota(jnp.int32, (g, w), 0)
    if reverse:
        cin = jnp.where(rowi == g - 1, carry, pltpu.roll(e, g - 1, 0))
        new_carry = e[0:1, :]
    else:
        cin = jnp.where(rowi == 0, carry, pltpu.roll(e, 1, 0))
        new_carry = e[g - 1:g, :]
    carry_scr[...] = new_carry

    if reverse:
        for k in range(SUB):
            out_ref[k * g:(k + 1) * g, :] = (hloc[k] + ploc[k] * cin).astype(BF16)
    else:
        for k in range(SUB):
            hk = hloc[k] + ploc[k] * cin + hb_ref[k * g:(k + 1) * g, :].astype(F32)
            for s in range(slabs):
                os_scr[s, pl.ds(k, g, stride=SUB), :] = hk[:, s * LANES:(s + 1) * LANES]
        tok = jnp.concatenate([os_scr[s] for s in range(slabs)], axis=1)
        out_ref[...] = (tok * gate_ref[...].astype(F32)).astype(BF16)

    @pl.when(i == nt - 1)
    def _():
        hfin_ref[...] = new_carry


def _rglru(proj, l, conv_w, conv_b, lru_wa, lru_ba, lru_wx, lru_bx, lru_lambda, h0_f, h0_b,
           *, batch, n, tt):
    w = LRU_BLOCK_DIM
    nt = n // tt
    hpt = tt // HALO
    nh = n // HALO
    xseg = 4 * LRU_BLOCKS
    gseg = 5 * LRU_BLOCKS
    slabs = w // LANES

    def specs(rev):
        d = 1 if rev else 0

        def t_of(i):
            return nt - 1 - i if rev else i

        def vec(arr_rows):
            return pl.BlockSpec((arr_rows, w), lambda b, cb, i: (0, cb))

        wspec = pl.BlockSpec((None, None, None, w, w), lambda b, cb, i: (l, d, cb, 0, 0))
        return [
            pl.BlockSpec((HALO, w), lambda b, cb, i: (b * nh + jnp.maximum(t_of(i) * hpt - 1, 0), xseg + cb)),
            pl.BlockSpec((tt, w), lambda b, cb, i: (b * nt + t_of(i), xseg + cb)),
            pl.BlockSpec((HALO, w), lambda b, cb, i: (b * nh + jnp.minimum((t_of(i) + 1) * hpt, nh - 1), xseg + cb)),
            vec(4), vec(1), wspec, vec(1), wspec, vec(1), vec(1),
            pl.BlockSpec((None, 1, w), lambda b, cb, i: (b, 0, cb)),
        ]

    stage = pltpu.VMEM((slabs, tt + 2 * SUB, LANES), F32)
    state_spec = pl.BlockSpec((None, 1, w), lambda b, cb, i: (b, 0, cb))
    state_shape = jax.ShapeDtypeStruct((batch, 1, D_MODEL), F32)
    grid = (batch, LRU_BLOCKS, nt)
    params = _cparams(("arbitrary", "arbitrary", "arbitrary"))

    def small(d):
        return (conv_w, conv_b.reshape(1, -1), lru_wa, lru_ba[d].reshape(1, -1), lru_wx,
                lru_bx[d].reshape(1, -1), lru_lambda[d].reshape(1, -1))

    hb, hfin_b = pl.pallas_call(
        functools.partial(_lru_kernel, reverse=True, tt=tt),
        grid=grid,
        in_specs=specs(True),
        out_specs=[pl.BlockSpec((tt, w), lambda b, cb, i: (b * nt + nt - 1 - i, cb)), state_spec],
        out_shape=[jax.ShapeDtypeStruct((batch * n, D_MODEL), BF16), state_shape],
        scratch_shapes=[pltpu.VMEM((1, w), F32), stage],
        compiler_params=params,
        name="lru_bwd",
    )(proj, proj, proj, *small(1), h0_b)

    y, hfin_f = pl.pallas_call(
        functools.partial(_lru_kernel, reverse=False, tt=tt),
        grid=grid,
        in_specs=specs(False) + [
            pl.BlockSpec((tt, w), lambda b, cb, i: (b * nt + i, cb)),
            pl.BlockSpec((tt, w), lambda b, cb, i: (b * nt + i, gseg + cb)),
        ],
        out_specs=[pl.BlockSpec((tt, w), lambda b, cb, i: (b * nt + i, cb)), state_spec],
        out_shape=[jax.ShapeDtypeStruct((batch * n, D_MODEL), BF16), state_shape],
        scratch_shapes=[pltpu.VMEM((1, w), F32), stage, pltpu.VMEM((slabs, tt, LANES), F32)],
        compiler_params=params,
        name="lru_fwd",
    )(proj, proj, proj, *small(0), h0_f, hb, proj)
    return y, hfin_f, hfin_b


MERGE_TM = 256


def _merge_kernel(yr_ref, yl_ref, sa_ref, sb_ref, x_ref, ga_ref, g_ref, wr_ref, wl_ref, wo_ref,
                  o_ref):
    pr = jnp.dot(yr_ref[...], wr_ref[...], preferred_element_type=F32)
    plru = jnp.dot(yl_ref[...], wl_ref[...], preferred_element_type=F32)
    y = sa_ref[...].astype(F32) * pr + sb_ref[...].astype(F32) * plru
    z = jnp.dot(y.astype(BF16), wo_ref[...], preferred_element_type=F32)
    o_ref[...] = x_ref[...] + ga_ref[...] * _rms(z, g_ref[...])


def _merge(y_r, y_l, proj, x2, mods, l, g_post, w_ret_o, w_lru_o, w_out, *, tiles_per_group, group0):
    m = x2.shape[0]
    tm = MERGE_TM
    row = lambda i: (i, 0)
    wspec = pl.BlockSpec((None, D_MODEL, D_MODEL), lambda i: (l, 0, 0), pipeline_mode=pl.Buffered(1))
    return pl.pallas_call(
        _merge_kernel,
        grid=(m // tm,),
        in_specs=[
            pl.BlockSpec((tm, D_MODEL), row),
            pl.BlockSpec((tm, D_MODEL), row),
            pl.BlockSpec((tm, D_MODEL), lambda i: (i, 6)),
            pl.BlockSpec((tm, D_MODEL), lambda i: (i, 7)),
            pl.BlockSpec((tm, D_MODEL), row),
            pl.BlockSpec((None, None, 1, D_MODEL), lambda i: (l, group0 + i // tiles_per_group, 0, 2)),
            pl.BlockSpec((1, D_MODEL), lambda i: (0, 0)),
            wspec, wspec, wspec,
        ],
        out_specs=pl.BlockSpec((tm, D_MODEL), row),
        out_shape=jax.ShapeDtypeStruct((m, D_MODEL), F32),
        compiler_params=_cparams(("arbitrary",)),
        name="merge",
    )(y_r, y_l, proj, proj, x2, mods, g_post.reshape(1, D_MODEL), w_ret_o, w_lru_o, w_out)


FFN_TH = 512
FFN_STEPS = FFN_HIDDEN // FFN_TH
FFN_EDGE_ROWS = 256


def _ffn_kernel(x_ref, sh_ref, sc_ref, ga_ref, gpre_ref, gpost_ref, wa_ref, wg_ref, wo_ref, o_ref,
                h_scr):
    j = pl.program_id(1)
    last_j = pl.num_programs(1) - 1
    rows = o_ref.shape[0]

    def step(first, last, sub_rows):
        for r in range(0, rows, sub_rows):
            rs = slice(r, r + sub_rows)
            if first:
                y = _rms(x_ref[rs, :], gpre_ref[...])
                h = (y * (1.0 + sc_ref[...]) + sh_ref[...]).astype(BF16)
                h_scr[rs, :] = h
            else:
                h = h_scr[rs, :]
            a = jnp.dot(h, wa_ref[...], preferred_element_type=F32)
            g = jnp.dot(h, wg_ref[...], preferred_element_type=F32)
            act = (a * _sigmoid(a) * g).astype(BF16)
            f = jnp.dot(act, wo_ref[...], preferred_element_type=F32)
            if not first:
                f = o_ref[rs, :] + f
            if last:
                f = x_ref[rs, :] + ga_ref[...] * _rms(f, gpost_ref[...])
            o_ref[rs, :] = f

    edge_rows = min(FFN_EDGE_ROWS, rows)
    pl.when(j == 0)(lambda: step(True, False, edge_rows))
    pl.when((j > 0) & (j < last_j))(lambda: step(False, False, rows))
    pl.when(j == last_j)(lambda: step(False, True, edge_rows))


def _ffn(x2, mods, l, g_pre, g_post, w_ffn_in, w_ffn_out, *, tm, tiles_per_group, group0):
    m = x2.shape[0]

    def mod_map(chunk):
        return lambda i, j: (l, group0 + i // tiles_per_group, 0, chunk)

    return pl.pallas_call(
        _ffn_kernel,
        grid=(m // tm, FFN_STEPS),
        in_specs=[
            pl.BlockSpec((tm, D_MODEL), lambda i, j: (i, 0)),
            pl.BlockSpec((None, None, 1, D_MODEL), mod_map(3)),
            pl.BlockSpec((None, None, 1, D_MODEL), mod_map(4)),
            pl.BlockSpec((None, None, 1, D_MODEL), mod_map(5)),
            pl.BlockSpec((1, D_MODEL), lambda i, j: (0, 0)),
            pl.BlockSpec((1, D_MODEL), lambda i, j: (0, 0)),
            pl.BlockSpec((None, D_MODEL, FFN_TH), lambda i, j: (l, 0, j)),
            pl.BlockSpec((None, D_MODEL, FFN_TH), lambda i, j: (l, 0, FFN_STEPS + j)),
            pl.BlockSpec((None, FFN_TH, D_MODEL), lambda i, j: (l, j, 0)),
        ],
        out_specs=pl.BlockSpec((tm, D_MODEL), lambda i, j: (i, 0)),
        out_shape=jax.ShapeDtypeStruct((m, D_MODEL), F32),
        scratch_shapes=[pltpu.VMEM((tm, D_MODEL), BF16)],
        compiler_params=_cparams(("arbitrary", "arbitrary")),
        name="ffn",
    )(x2, mods, mods, mods, g_pre.reshape(1, D_MODEL), g_post.reshape(1, D_MODEL),
      w_ffn_in, w_ffn_in, w_ffn_out)


def _rope_tables(n):
    t = jnp.arange(n, dtype=jnp.int32)
    row = (t // GRID_W).astype(F32)
    col = (t % GRID_W).astype(F32)
    inv = ROPE_BASE ** (-jnp.arange(ROPE_FREQS, dtype=F32) / ROPE_FREQS)
    ar = row[:, None] * inv[None, :]
    ac = col[:, None] * inv[None, :]
    cos = jnp.concatenate([jnp.cos(ar), jnp.cos(ar), jnp.cos(ac), jnp.cos(ac)], axis=1)
    sin = jnp.concatenate([-jnp.sin(ar), jnp.sin(ar), -jnp.sin(ac), jnp.sin(ac)], axis=1)
    return cos, sin


def _pick_tile(n, pref):
    t = pref
    while n % t:
        t //= 2
    return t


def kernel(x, c, ctx, c_ctx, w_mod, b_mod, g_pre_mix, g_post_mix, g_pre_ffn, g_post_ffn, w_in, b_in,
           ret_decay, ret_gn, conv_w, conv_b, lru_wa, lru_ba, lru_wx, lru_bx, lru_lambda, w_ret_o,
           w_lru_o, w_out, w_ffn_in, w_ffn_out):
    batch, n, d = x.shape
    n_ctx = ctx.shape[1]
    depth = w_mod.shape[0]
    assert d == D_MODEL and n % RET_CHUNK == 0 and n_ctx % RET_CHUNK == 0
    assert RET_CHUNK == HEAD_DIM

    mod_rows = SUB * ((batch + 1 + SUB - 1) // SUB)
    c_rows = jnp.zeros((mod_rows, D_MODEL), F32).at[:batch].set(c).at[batch].set(c_ctx)
    mods = _modulation(c_rows, w_mod, b_mod).reshape(depth, mod_rows, 1, 6 * D_MODEL)

    cos_tab, sin_tab = _rope_tables(n)
    cos_ctx = jnp.ones((batch * n_ctx, HEAD_DIM), F32)
    sin_ctx = jnp.zeros((batch * n_ctx, HEAD_DIM), F32)

    tm_lat = _pick_tile(n, 1024)
    tm_ffn = _pick_tile(n, 512)
    tm_ctx = batch * n_ctx
    tt_lat = _pick_tile(n, 2048)
    cpb_lat = _pick_tile(n // RET_CHUNK, 8)

    x2 = x.reshape(batch * n, D_MODEL)
    ctx2 = ctx.reshape(batch * n_ctx, D_MODEL)

    w_in = w_in.astype(BF16)
    lru_wa, lru_wx = (0.5 * lru_wa).astype(BF16), (0.5 * lru_wx).astype(BF16)
    w_ret_o, w_lru_o, w_out = w_ret_o.astype(BF16), w_lru_o.astype(BF16), w_out.astype(BF16)
    w_ffn_in, w_ffn_out = w_ffn_in.astype(BF16), w_ffn_out.astype(BF16)

    zeros_s = jnp.zeros((batch, RET_HEADS, HEAD_DIM, HEAD_DIM), F32)
    zeros_h = jnp.zeros((batch, 1, D_MODEL), F32)

    for l in range(depth):
        with_ctx = l < depth - 1
        lru_args = (conv_w[l], conv_b[l], lru_wa, lru_ba[l], lru_wx, lru_bx[l], lru_lambda[l])

        proj_c = _in_proj(ctx2, mods, l, g_pre_mix[l], w_in, b_in[l], cos_ctx, sin_ctx, tm=tm_ctx,
                          tiles_per_group=1, group0=batch, use_rope=False)
        yr_c, s_f, s_b = _retention(proj_c, ret_decay[l], ret_gn[l], zeros_s, zeros_s,
                                    batch=batch, n=n_ctx, cpb=n_ctx // RET_CHUNK)
        yl_c, h_f, h_b = _rglru(proj_c, l, *lru_args, zeros_h, zeros_h, batch=batch, n=n_ctx, tt=n_ctx)

        proj = _in_proj(x2, mods, l, g_pre_mix[l], w_in, b_in[l], cos_tab, sin_tab, tm=tm_lat,
                        tiles_per_group=n // tm_lat, group0=0, use_rope=True)
        y_r, _, _ = _retention(proj, ret_decay[l], ret_gn[l], s_f, s_b, batch=batch, n=n, cpb=cpb_lat)
        y_l, _, _ = _rglru(proj, l, *lru_args, h_f, h_b, batch=batch, n=n, tt=tt_lat)

        x2 = _merge(y_r, y_l, proj, x2, mods, l, g_post_mix[l], w_ret_o, w_lru_o, w_out,
                    tiles_per_group=n // MERGE_TM, group0=0)
        x2 = _ffn(x2, mods, l, g_pre_ffn[l], g_post_ffn[l], w_ffn_in, w_ffn_out, tm=tm_ffn,
                  tiles_per_group=n // tm_ffn, group0=0)
        if with_ctx:
            ctx2 = _merge(yr_c, yl_c, proj_c, ctx2, mods, l, g_post_mix[l], w_ret_o, w_lru_o, w_out,
                          tiles_per_group=batch * n_ctx // MERGE_TM, group0=batch)
            ctx2 = _ffn(ctx2, mods, l, g_pre_ffn[l], g_post_ffn[l], w_ffn_in, w_ffn_out, tm=tm_ctx,
                        tiles_per_group=1, group0=batch)
    return x2.reshape(batch, n, D_MODEL)
```
